```python
import jax, jax.numpy as jnp
from jax import lax
import numpy as np

D_MODEL = 1024
BATCH = 8
SEQ = 2048
DEPTH = 1
DEC_BATCH = 128
DEC_SEQ = 8
PAST_LEN = 16384
PAGE_SIZE = 128

D_A = D_MODEL // 2
HEAD_DIM = 128
N_HEADS = D_A // HEAD_DIM
D_B = D_MODEL // 2
CONV_W = 3
D_FF = 2816
CHUNK = 32
EPS = 1e-6
SPLIT_POINTS = (D_A, 2 * D_A, 3 * D_A, 4 * D_A, 4 * D_A + D_B, 4 * D_A + 2 * D_B,
                4 * D_A + 3 * D_B, 4 * D_A + 3 * D_B + D_MODEL)
N_IN = 4 * D_A + 3 * D_B + 2 * D_MODEL

kernel_name = "hgrn2_shortconv_gated_macaron_step"


def rmsnorm(x, g):
    xf = x.astype(jnp.float32)
    y = xf * lax.rsqrt(jnp.mean(xf * xf, axis=-1, keepdims=True) + EPS)
    return (y * g.astype(jnp.float32)).astype(x.dtype)


def swiglu(x, w1, w3, w2):
    return (jax.nn.silu(x @ w1) * (x @ w3)) @ w2


def hgrn2_chunked(q, k, v, logf, s0):
    bsz, L, H, dk = q.shape
    dv = v.shape[-1]
    C = CHUNK if L % CHUNK == 0 else L
    N = L // C
    q, k, v, logf = (t.reshape(bsz, N, C, H, t.shape[-1]) for t in (q, k, v, logf))
    b = jnp.cumsum(logf, axis=2)
    b_last = b[:, :, -1:]
    q_d = q * jnp.exp(b)
    k_d = k * jnp.exp(-b)
    scores = jnp.einsum('bnchk,bnshk->bnhcs', q_d, k_d)
    causal = jnp.tril(jnp.ones((C, C), dtype=bool))
    scores = jnp.where(causal, scores, 0.0)
    o_intra = jnp.einsum('bnhcs,bnshv->bnchv', scores, v)
    ds = jnp.einsum('bnchk,bnchv->bnhkv', k * jnp.exp(b_last - b), v)
    decay = jnp.exp(b_last[:, :, 0])

    def step(s, inp):
        dec, d = inp
        return dec[..., None] * s + d, s

    s_final, s_starts = lax.scan(step, s0, (jnp.moveaxis(decay, 1, 0), jnp.moveaxis(ds, 1, 0)))
    s_starts = jnp.moveaxis(s_starts, 0, 1)
    o_inter = jnp.einsum('bnchk,bnhkv->bnchv', q_d, s_starts)
    return (o_intra + o_inter).reshape(bsz, L, H, dv), s_final


def parallel_mixer(h, s_hgrn, s_conv, lb, w_in, conv_w, g_hgrn_out, w_a_out, w_b_out, w_o):
    bsz, L, _ = h.shape
    f32 = jnp.float32
    p = h @ w_in
    q, fz, iv, og, bg, cg, vv, ga, gb = jnp.split(p, SPLIT_POINTS, axis=-1)

    heads = lambda t: t.astype(f32).reshape(bsz, L, N_HEADS, HEAD_DIM)
    z = heads(fz)
    lbh = lb.astype(f32).reshape(N_HEADS, HEAD_DIM)
    f = lbh + (1.0 - lbh) * jax.nn.sigmoid(z)
    k_in = (1.0 - lbh) * jax.nn.sigmoid(-z)
    o, s_hgrn_new = hgrn2_chunked(heads(q), k_in, heads(iv), jnp.log(f), s_hgrn.astype(f32))
    o = o * lax.rsqrt(jnp.mean(o * o, axis=-1, keepdims=True) + EPS)
    o = o * g_hgrn_out.astype(f32).reshape(N_HEADS, HEAD_DIM)
    o = o.reshape(bsz, L, D_A) * jax.nn.silu(og.astype(f32))
    y_a = o.astype(h.dtype) @ w_a_out

    u = cg * vv
    up = jnp.concatenate([s_conv.astype(u.dtype), u], axis=1)
    conv = sum(conv_w[j] * up[:, j:j + L] for j in range(CONV_W))
    y_b = (bg * conv) @ w_b_out
    s_conv_new = up[:, L:]

    merged = jax.nn.sigmoid(ga) * y_a + jax.nn.sigmoid(gb) * y_b
    return merged @ w_o, s_hgrn_new.astype(s_hgrn.dtype), s_conv_new.astype(s_conv.dtype)


def trunk(x, st_h, st_c, lower_bound_logits, g_ffn1, w1_ffn1, w3_ffn1, w2_ffn1, g_mix, w_in,
          conv_w, g_hgrn_out, w_a_out, w_b_out, w_o, g_ffn2, w1_ffn2, w3_ffn2, w2_ffn2, g_final):
    lb_all = jnp.cumsum(jax.nn.softmax(lower_bound_logits.astype(jnp.float32), axis=0), axis=0)
    new_h, new_c = [], []
    for l in range(DEPTH):
        x = x + 0.5 * swiglu(rmsnorm(x, g_ffn1[l]), w1_ffn1[l], w3_ffn1[l], w2_ffn1[l])
        mix, sh, sc = parallel_mixer(rmsnorm(x, g_mix[l]), st_h[l], st_c[l], lb_all[l], w_in[l],
                                     conv_w[l], g_hgrn_out[l], w_a_out[l], w_b_out[l], w_o[l])
        x = x + mix
        x = x + 0.5 * swiglu(rmsnorm(x, g_ffn2[l]), w1_ffn2[l], w3_ffn2[l], w2_ffn2[l])
        new_h.append(sh)
        new_c.append(sc)
    return rmsnorm(x, g_final), jnp.stack(new_h), jnp.stack(new_c)


def setup_inputs(seed: int = 0) -> dict:
    key = jax.random.key(seed)
    ks = jax.random.split(key, 32)
    nrm = lambda k, shape, scale: jax.random.normal(k, shape, jnp.float32) * scale
    gain = lambda k, shape: 1.0 + 0.02 * jax.random.normal(k, shape, jnp.float32)
    return {
        "x_prompt": nrm(ks[0], (BATCH, SEQ, D_MODEL), 1.0),
        "x_sample": nrm(ks[1], (DEC_BATCH, DEC_SEQ, D_MODEL), 1.0),
        "state_hgrn": nrm(ks[2], (DEPTH, DEC_BATCH, N_HEADS, HEAD_DIM, HEAD_DIM), 0.5),
        "state_conv": nrm(ks[3], (DEPTH, DEC_BATCH, CONV_W - 1, D_B), 1.0),
        "lower_bound_logits": nrm(ks[4], (DEPTH + 1, D_A), 0.1),
        "g_ffn1": gain(ks[5], (DEPTH, D_MODEL)),
        "w1_ffn1": nrm(ks[6], (DEPTH, D_MODEL, D_FF), D_MODEL ** -0.5),
        "w3_ffn1": nrm(ks[7], (DEPTH, D_MODEL, D_FF), D_MODEL ** -0.5),
        "w2_ffn1": nrm(ks[8], (DEPTH, D_FF, D_MODEL), D_FF ** -0.5),
        "g_mix": gain(ks[9], (DEPTH, D_MODEL)),
        "w_in": nrm(ks[10], (DEPTH, D_MODEL, N_IN), D_MODEL ** -0.5),
        "conv_w": nrm(ks[11], (DEPTH, CONV_W, D_B), CONV_W ** -0.5),
        "g_hgrn_out": gain(ks[12], (DEPTH, D_A)),
        "w_a_out": nrm(ks[13], (DEPTH, D_A, D_MODEL), D_A ** -0.5),
        "w_b_out": nrm(ks[14], (DEPTH, D_B, D_MODEL), D_B ** -0.5),
        "w_o": nrm(ks[15], (DEPTH, D_MODEL, D_MODEL), D_MODEL ** -0.5),
        "g_ffn2": gain(ks[16], (DEPTH, D_MODEL)),
        "w1_ffn2": nrm(ks[17], (DEPTH, D_MODEL, D_FF), D_MODEL ** -0.5),
        "w3_ffn2": nrm(ks[18], (DEPTH, D_MODEL, D_FF), D_MODEL ** -0.5),
        "w2_ffn2": nrm(ks[19], (DEPTH, D_FF, D_MODEL), D_FF ** -0.5),
        "g_final": gain(ks[20], (D_MODEL,)),
    }


def reference(x_prompt, x_sample, state_hgrn, state_conv, lower_bound_logits, g_ffn1, w1_ffn1,
              w3_ffn1, w2_ffn1, g_mix, w_in, conv_w, g_hgrn_out, w_a_out, w_b_out, w_o, g_ffn2,
              w1_ffn2, w3_ffn2, w2_ffn2, g_final):
    weights = (lower_bound_logits, g_ffn1, w1_ffn1, w3_ffn1, w2_ffn1, g_mix, w_in, conv_w,
               g_hgrn_out, w_a_out, w_b_out, w_o, g_ffn2, w1_ffn2, w3_ffn2, w2_ffn2, g_final)
    n_prompt = x_prompt.shape[0]
    zero_h = jnp.zeros((DEPTH, n_prompt) + state_hgrn.shape[2:], state_hgrn.dtype)
    zero_c = jnp.zeros((DEPTH, n_prompt) + state_conv.shape[2:], state_conv.dtype)
    y_prompt, state_hgrn_prompt, state_conv_prompt = trunk(x_prompt, zero_h, zero_c, *weights)
    y_sample, state_hgrn_sample, state_conv_sample = trunk(x_sample, state_hgrn, state_conv, *weights)
    return (y_prompt, y_sample, state_hgrn_prompt, state_conv_prompt, state_hgrn_sample, state_conv_sample)
```

```python
import functools

import jax
import jax.numpy as jnp
from jax import lax
from jax.experimental import pallas as pl
from jax.experimental.pallas import tpu as pltpu

F32 = jnp.float32
BF16 = jnp.bfloat16

D_MODEL = 1024
D_A = 512
HEAD_DIM = 128
N_HEADS = 4
D_B = 512
CONV_W = 3
D_FF = 2816
CHUNK = 32
EPS = 1e-6

LANES = 128
VMEM_LIMIT_BYTES = 56 * 1024 * 1024

FFN_ROWS = 512
FFN_COLS = 256
MIX_ROWS = 256
DEC_SEQS = 16


def _dot(a, b):
    return jnp.dot(a, b, preferred_element_type=F32)


def _dot_nt(a, b):
    return lax.dot_general(a, b, (((1,), (1,)), ((), ())), preferred_element_type=F32)


def _rmsnorm(x, g):
    return x * lax.rsqrt(jnp.mean(x * x, axis=-1, keepdims=True) + EPS) * g


def _const_spec(shape):
    nd = len(shape)
    return pl.BlockSpec(shape, lambda *_: (0,) * nd, pipeline_mode=pl.Buffered(1))


def _ffn_kernel(x_ref, g_ref, w1_ref, w3_ref, w2_ref, gf_ref, o_ref, *, final_norm):
    x = x_ref[...]
    h = _rmsnorm(x, g_ref[...]).astype(BF16)
    acc = None
    for c in range(D_FF // FFN_COLS):
        sl = slice(c * FFN_COLS, (c + 1) * FFN_COLS)
        a = _dot(h, w1_ref[:, sl])
        b = _dot(h, w3_ref[:, sl])
        act = (a * jax.nn.sigmoid(a) * b).astype(BF16)
        part = _dot(act, w2_ref[sl, :])
        acc = part if acc is None else acc + part
    y = x + 0.5 * acc
    if final_norm:
        y = _rmsnorm(y, gf_ref[...])
    o_ref[...] = y


def _ffn(x2d, g, w1, w3, w2, g_final, *, final_norm):
    n = x2d.shape[0]
    rows = FFN_ROWS
    assert n % rows == 0
    row_spec = pl.BlockSpec((rows, D_MODEL), lambda i: (i, 0))
    return pl.pallas_call(
        functools.partial(_ffn_kernel, final_norm=final_norm),
        grid=(n // rows,),
        in_specs=[row_spec, _const_spec((1, D_MODEL)), _const_spec((D_MODEL, D_FF)),
                  _const_spec((D_MODEL, D_FF)), _const_spec((D_FF, D_MODEL)), _const_spec((1, D_MODEL))],
        out_specs=row_spec,
        out_shape=jax.ShapeDtypeStruct((n, D_MODEL), F32),
        compiler_params=pltpu.CompilerParams(dimension_semantics=("arbitrary",),
                                             vmem_limit_bytes=VMEM_LIMIT_BYTES),
        name="ffn_final" if final_norm else "ffn",
    )(x2d, g, w1, w3, w2, g_final)


def _lower_bound(lbl):
    m = jnp.max(lbl, axis=0, keepdims=True)
    e = jnp.exp(lbl - m)
    return e[0:1, :] / jnp.sum(e, axis=0, keepdims=True)


def _split3(x):
    hi = x.astype(BF16)
    r1 = x - hi.astype(F32)
    mid = r1.astype(BF16)
    lo = (r1 - mid.astype(F32)).astype(BF16)
    return hi, mid, lo


def _chunk_ids(rows, chunk):
    shift = chunk.bit_length() - 1
    assert 1 << shift == chunk
    r = lax.broadcasted_iota(jnp.int32, (rows, rows), 0)
    c = lax.broadcasted_iota(jnp.int32, (rows, rows), 1)
    same = lax.shift_right_logical(r, shift) == lax.shift_right_logical(c, shift)
    return same, same & (c <= r)


def _gates(h, win_ref, lb, rows, chunk, same, tril):
    pq = _dot(h, win_ref[:, 0 * D_A:1 * D_A])
    pf = _dot(h, win_ref[:, 1 * D_A:2 * D_A])
    pv = _dot(h, win_ref[:, 2 * D_A:3 * D_A])
    og = _dot(h, win_ref[:, 3 * D_A:4 * D_A])
    one_m = 1.0 - lb
    f = lb + one_m * jax.nn.sigmoid(pf)
    k_in = one_m * jax.nn.sigmoid(-pf)
    logf = jnp.log(f)
    lhs = jnp.concatenate([tril, same], axis=0).astype(BF16)
    cs = _dot(lhs, jnp.concatenate(_split3(logf), axis=1))
    cs = cs[:, 0:D_A] + cs[:, D_A:2 * D_A] + cs[:, 2 * D_A:3 * D_A]
    b, blast = cs[0:rows], cs[rows:2 * rows]
    q_d = pq * jnp.exp(b)
    k_d = k_in * jnp.exp(-b)
    k_w = k_in * jnp.exp(blast - b)
    return q_d, k_d, k_w, pv, og, blast


def _intra(q_d_h, k_d_h, v_h, tril):
    scores = jnp.where(tril, _dot_nt(q_d_h.astype(BF16), k_d_h.astype(BF16)), 0.0)
    return _dot(scores.astype(BF16), v_h.astype(BF16))


def _head_out(o, gh_h, og_h):
    o = o * lax.rsqrt(jnp.mean(o * o, axis=-1, keepdims=True) + EPS)
    return o * gh_h * (og_h * jax.nn.sigmoid(og_h))


def _chunk_blocks(x_h, rows, chunk):
    shift = chunk.bit_length() - 1
    rid = lax.shift_right_logical(lax.broadcasted_iota(jnp.int32, (rows, HEAD_DIM), 0), shift)
    xb = x_h.astype(BF16)
    zero = jnp.zeros_like(xb)
    return jnp.concatenate([jnp.where(rid == n, xb, zero) for n in range(rows // chunk)], axis=1)


def _merge_out(x, h, y_a_in, z, win_ref, wa_ref, wb_ref, wo_ref):
    base = 4 * D_A + 3 * D_B
    ga = _dot(h, win_ref[:, base:base + D_MODEL])
    gb = _dot(h, win_ref[:, base + D_MODEL:base + 2 * D_MODEL])
    y_a = _dot(y_a_in.astype(BF16), wa_ref[...])
    y_b = _dot(z.astype(BF16), wb_ref[...])
    merged = jax.nn.sigmoid(ga) * y_a + jax.nn.sigmoid(gb) * y_b
    return x + _dot(merged.astype(BF16), wo_ref[...])


def _mix_kernel(x_ref, lbl_ref, gm_ref, win_ref, cw_ref, gh_ref, wa_ref, wb_ref, wo_ref,
                xo_ref, so_ref, co_ref, st_scr, cv_scr):
    rows, chunk = MIX_ROWS, CHUNK
    t = pl.program_id(1)

    @pl.when(t == 0)
    def _():
        st_scr[...] = jnp.zeros_like(st_scr)
        cv_scr[...] = jnp.zeros_like(cv_scr)

    x = x_ref[0]
    h = _rmsnorm(x, gm_ref[...]).astype(BF16)
    lb = _lower_bound(lbl_ref[...])
    same, tril = _chunk_ids(rows, chunk)
    q_d, k_d, k_w, v, og, blast = _gates(h, win_ref, lb, rows, chunk, same, tril)
    gh = gh_ref[...]

    heads = []
    for hd in range(N_HEADS):
        sl = slice(hd * HEAD_DIM, (hd + 1) * HEAD_DIM)
        o_intra = _intra(q_d[:, sl], k_d[:, sl], v[:, sl], tril)
        ds_t = _dot(v[:, sl].T.astype(BF16), _chunk_blocks(k_w[:, sl], rows, chunk))
        s_t = st_scr[hd]
        q_h = q_d[:, sl].astype(BF16)
        bl_h = blast[:, sl]
        inter = []
        for n in range(rows // chunk):
            inter.append(_dot_nt(q_h[n * chunk:(n + 1) * chunk], s_t.astype(BF16)))
            decay = jnp.exp(bl_h[n * chunk:n * chunk + 1, :])
            s_t = s_t * decay + ds_t[:, n * HEAD_DIM:(n + 1) * HEAD_DIM]
        st_scr[hd] = s_t
        o = o_intra + jnp.concatenate(inter, axis=0)
        heads.append(_head_out(o, gh[:, sl], og[:, sl]))
    y_a_in = jnp.concatenate(heads, axis=1)

    bg = _dot(h, win_ref[:, 4 * D_A:4 * D_A + D_B])
    cg = _dot(h, win_ref[:, 4 * D_A + D_B:4 * D_A + 2 * D_B])
    vv = _dot(h, win_ref[:, 4 * D_A + 2 * D_B:4 * D_A + 3 * D_B])
    u = cg * vv
    prev = cv_scr[...]
    rid = lax.broadcasted_iota(jnp.int32, (rows, D_B), 0)
    u_m1 = jnp.where(rid == 0, prev[7:8, :], pltpu.roll(u, 1, 0))
    u_m2 = jnp.where(rid == 0, prev[6:7, :], jnp.where(rid == 1, prev[7:8, :], pltpu.roll(u, 2, 0)))
    cw = cw_ref[...]
    conv = cw[0:1, :] * u_m2 + cw[1:2, :] * u_m1 + cw[2:3, :] * u
    cv_scr[...] = u[rows - 8:rows, :]

    xo_ref[0] = _merge_out(x, h, y_a_in, bg * conv, win_ref, wa_ref, wb_ref, wo_ref)

    @pl.when(t == pl.num_programs(1) - 1)
    def _():
        for hd in range(N_HEADS):
            so_ref[0, 0, hd] = st_scr[hd].T
        co_ref[0, 0] = u[rows - 2:rows, :]


def _mix(x, lbl, gm, win, cw, gh, wa, wb, wo):
    bsz, seq, _ = x.shape
    rows = MIX_ROWS
    assert seq % rows == 0
    n_in = win.shape[1]
    x_spec = pl.BlockSpec((1, rows, D_MODEL), lambda b, t: (b, t, 0))
    return pl.pallas_call(
        _mix_kernel,
        grid=(bsz, seq // rows),
        in_specs=[x_spec, _const_spec(lbl.shape), _const_spec((1, D_MODEL)), _const_spec((D_MODEL, n_in)),
                  _const_spec((CONV_W, D_B)), _const_spec((1, D_A)), _const_spec((D_A, D_MODEL)),
                  _const_spec((D_B, D_MODEL)), _const_spec((D_MODEL, D_MODEL))],
        out_specs=[x_spec,
                   pl.BlockSpec((1, 1, N_HEADS, HEAD_DIM, HEAD_DIM), lambda b, t: (0, b, 0, 0, 0)),
                   pl.BlockSpec((1, 1, CONV_W - 1, D_B), lambda b, t: (0, b, 0, 0))],
        out_shape=[jax.ShapeDtypeStruct(x.shape, F32),
                   jax.ShapeDtypeStruct((1, bsz, N_HEADS, HEAD_DIM, HEAD_DIM), F32),
                   jax.ShapeDtypeStruct((1, bsz, CONV_W - 1, D_B), F32)],
        scratch_shapes=[pltpu.VMEM((N_HEADS, HEAD_DIM, HEAD_DIM), F32), pltpu.VMEM((8, D_B), F32)],
        compiler_params=pltpu.CompilerParams(dimension_semantics=("arbitrary", "arbitrary"),
                                             vmem_limit_bytes=VMEM_LIMIT_BYTES),
        name="mix",
    )(x, lbl, gm, win, cw, gh, wa, wb, wo)


def _mix_dec_kernel(x_ref, s0_ref, c0_ref, lbl_ref, gm_ref, win_ref, cw_ref, gh_ref, wa_ref, wb_ref, wo_ref,
                    xo_ref, so_ref, co_ref, *, dec_len):
    nseq, chunk = DEC_SEQS, dec_len
    rows = nseq * chunk

    x = x_ref[...]
    h = _rmsnorm(x, gm_ref[...]).astype(BF16)
    lb = _lower_bound(lbl_ref[...])
    same, tril = _chunk_ids(rows, chunk)
    q_d, k_d, k_w, v, og, blast = _gates(h, win_ref, lb, rows, chunk, same, tril)
    gh = gh_ref[...]

    heads = []
    for hd in range(N_HEADS):
        sl = slice(hd * HEAD_DIM, (hd + 1) * HEAD_DIM)
        o_intra = _intra(q_d[:, sl], k_d[:, sl], v[:, sl], tril)
        ds = _dot(k_w[:, sl].T.astype(BF16), _chunk_blocks(v[:, sl], rows, chunk))
        decay_t = jnp.exp(blast[:, sl]).T
        q_h = q_d[:, sl].astype(BF16)
        inter = []
        for j in range(nseq):
            s0 = s0_ref[0, j, hd]
            inter.append(_dot(q_h[j * chunk:(j + 1) * chunk], s0.astype(BF16)))
            so_ref[0, j, hd] = (s0 * decay_t[:, j * chunk:j * chunk + 1]
                                + ds[:, j * HEAD_DIM:(j + 1) * HEAD_DIM])
        o = o_intra + jnp.concatenate(inter, axis=0)
        heads.append(_head_out(o, gh[:, sl], og[:, sl]))
    y_a_in = jnp.concatenate(heads, axis=1)

    bg = _dot(h, win_ref[:, 4 * D_A:4 * D_A + D_B])
    cg = _dot(h, win_ref[:, 4 * D_A + D_B:4 * D_A + 2 * D_B])
    vv = _dot(h, win_ref[:, 4 * D_A + 2 * D_B:4 * D_A + 3 * D_B])
    u = cg * vv
    c0 = c0_ref[0]
    tid = lax.broadcasted_iota(jnp.int32, (nseq, chunk, D_B), 1)
    u3 = u.reshape(nseq, chunk, D_B)
    r1 = pltpu.roll(u, 1, 0).reshape(nseq, chunk, D_B)
    r2 = pltpu.roll(u, 2, 0).reshape(nseq, chunk, D_B)
    u_m1 = jnp.where(tid == 0, c0[:, 1:2, :], r1)
    u_m2 = jnp.where(tid == 0, c0[:, 0:1, :], jnp.where(tid == 1, c0[:, 1:2, :], r2))
    cw = cw_ref[...]
    conv = cw[0:1, :] * u_m2 + cw[1:2, :] * u_m1 + cw[2:3, :] * u3
    co_ref[0] = u3[:, chunk - 2:chunk, :]
    z = bg * conv.reshape(rows, D_B)

    xo_ref[...] = _merge_out(x, h, y_a_in, z, win_ref, wa_ref, wb_ref, wo_ref)


def _mix_dec(x, s0, c0, lbl, gm, win, cw, gh, wa, wb, wo):
    bsz, dec_len, _ = x.shape
    assert dec_len % 8 == 0 and dec_len >= CONV_W - 1 and bsz % DEC_SEQS == 0
    rows = DEC_SEQS * dec_len
    n_in = win.shape[1]
    x_spec = pl.BlockSpec((rows, D_MODEL), lambda i: (i, 0))
    s_spec = pl.BlockSpec((1, DEC_SEQS, N_HEADS, HEAD_DIM, HEAD_DIM), lambda i: (0, i, 0, 0, 0))
    c_spec = pl.BlockSpec((1, DEC_SEQS, CONV_W - 1, D_B), lambda i: (0, i, 0, 0))
    xo, so, co = pl.pallas_call(
        functools.partial(_mix_dec_kernel, dec_len=dec_len),
        grid=(bsz // DEC_SEQS,),
        in_specs=[x_spec, s_spec, c_spec, _const_spec(lbl.shape), _const_spec((1, D_MODEL)),
                  _const_spec((D_MODEL, n_in)), _const_spec((CONV_W, D_B)), _const_spec((1, D_A)),
                  _const_spec((D_A, D_MODEL)), _const_spec((D_B, D_MODEL)), _const_spec((D_MODEL, D_MODEL))],
        out_specs=[x_spec, s_spec, c_spec],
        out_shape=[jax.ShapeDtypeStruct((bsz * dec_len, D_MODEL), F32),
                   jax.ShapeDtypeStruct(s0.shape, F32), jax.ShapeDtypeStruct(c0.shape, F32)],
        compiler_params=pltpu.CompilerParams(dimension_semantics=("arbitrary",),
                                             vmem_limit_bytes=VMEM_LIMIT_BYTES),
        name="mix_dec",
    )(x.reshape(bsz * dec_len, D_MODEL), s0, c0, lbl, gm, win, cw, gh, wa, wb, wo)
    return xo.reshape(x.shape), so, co


def kernel(x_prompt, x_sample, state_hgrn, state_conv, lower_bound_logits, g_ffn1, w1_ffn1, w3_ffn1, w2_ffn1,
           g_mix, w_in, conv_w, g_hgrn_out, w_a_out, w_b_out, w_o, g_ffn2, w1_ffn2, w3_ffn2, w2_ffn2, g_final):
    assert state_hgrn.shape[0] == 1, "single-layer trunk"
    bf = lambda w: w[0].astype(BF16)
    ffn1 = (g_ffn1, bf(w1_ffn1), bf(w3_ffn1), bf(w2_ffn1))
    ffn2 = (g_ffn2, bf(w1_ffn2), bf(w3_ffn2), bf(w2_ffn2))
    gfin = g_final.reshape(1, D_MODEL)
    mixw = (lower_bound_logits, g_mix, bf(w_in), conv_w[0], g_hgrn_out, bf(w_a_out), bf(w_b_out), bf(w_o))

    def ffn(x, weights, final_norm):
        shape = x.shape
        y = _ffn(x.reshape(-1, D_MODEL), *weights, gfin, final_norm=final_norm)
        return y.reshape(shape)

    xp = ffn(x_prompt, ffn1, False)
    xp, sh_p, sc_p = _mix(xp, *mixw)
    yp = ffn(xp, ffn2, True)

    xs = ffn(x_sample, ffn1, False)
    xs, sh_s, sc_s = _mix_dec(xs, state_hgrn, state_conv, *mixw)
    ys = ffn(xs, ffn2, True)
    return yp, ys, sh_p, sc_p, sh_s, sc_s
```

```python
import functools

import jax
import jax.numpy as jnp
from jax import lax
from jax.experimental import pallas as pl
from jax.experimental.pallas import tpu as pltpu

F32 = jnp.float32
BF16 = jnp.bfloat16

D_MODEL = 1024
D_A = 512
HEAD_DIM = 128
N_HEADS = 4
D_B = 512
CONV_W = 3
D_FF = 2816
CHUNK = 32
EPS = 1e-6

LANES = 128
VMEM_LIMIT_BYTES = 56 * 1024 * 1024

FFN_ROWS = 512
FFN_COLS = 256
MIX_ROWS = 256
MIX_SEQS = 2
DEC_SEQS = 16


def _dot(a, b):
    return jnp.dot(a, b, preferred_element_type=F32)


def _dot_nt(a, b):
    return lax.dot_general(a, b, (((1,), (1,)), ((), ())), preferred_element_type=F32)


def _rmsnorm(x, g):
    return x * lax.rsqrt(jnp.mean(x * x, axis=-1, keepdims=True) + EPS) * g


def _const_spec(shape):
    nd = len(shape)
    return pl.BlockSpec(shape, lambda *_: (0,) * nd, pipeline_mode=pl.Buffered(1))


def _ffn_kernel(x_ref, g_ref, w1_ref, w3_ref, w2_ref, gf_ref, o_ref, *, final_norm):
    x = x_ref[...]
    h = _rmsnorm(x, g_ref[...]).astype(BF16)
    acc = None
    for c in range(D_FF // FFN_COLS):
        sl = slice(c * FFN_COLS, (c + 1) * FFN_COLS)
        a = _dot(h, w1_ref[:, sl].astype(BF16))
        b = _dot(h, w3_ref[:, sl].astype(BF16))
        act = (a * jax.nn.sigmoid(a) * b).astype(BF16)
        part = _dot(act, w2_ref[sl, :].astype(BF16))
        acc = part if acc is None else acc + part
    y = x + 0.5 * acc
    if final_norm:
        y = _rmsnorm(y, gf_ref[...])
    o_ref[...] = y


def _ffn(x2d, g, w1, w3, w2, g_final, *, final_norm):
    n = x2d.shape[0]
    rows = FFN_ROWS
    assert n % rows == 0
    row_spec = pl.BlockSpec((rows, D_MODEL), lambda i: (i, 0))
    return pl.pallas_call(
        functools.partial(_ffn_kernel, final_norm=final_norm),
        grid=(n // rows,),
        in_specs=[row_spec, _const_spec((1, D_MODEL)), _const_spec((D_MODEL, D_FF)),
                  _const_spec((D_MODEL, D_FF)), _const_spec((D_FF, D_MODEL)), _const_spec((1, D_MODEL))],
        out_specs=row_spec,
        out_shape=jax.ShapeDtypeStruct((n, D_MODEL), F32),
        compiler_params=pltpu.CompilerParams(dimension_semantics=("arbitrary",),
                                             vmem_limit_bytes=VMEM_LIMIT_BYTES),
        name="ffn_final" if final_norm else "ffn",
    )(x2d, g, w1, w3, w2, g_final)


def _lower_bound(lbl):
    m = jnp.max(lbl, axis=0, keepdims=True)
    e = jnp.exp(lbl - m)
    return e[0:1, :] / jnp.sum(e, axis=0, keepdims=True)


def _causal_in_chunk(rows, chunk):
    shift = chunk.bit_length() - 1
    assert 1 << shift == chunk
    r = lax.broadcasted_iota(jnp.int32, (rows, rows), 0)
    c = lax.broadcasted_iota(jnp.int32, (rows, rows), 1)
    return (lax.shift_right_logical(r, shift) == lax.shift_right_logical(c, shift)) & (c <= r)


def _chunk_cumsum(x, chunk):
    pos = lax.broadcasted_iota(jnp.int32, x.shape, 0) & (chunk - 1)
    step = 1
    while step < chunk:
        x = x + jnp.where(pos >= step, pltpu.roll(x, step, 0), 0.0)
        step *= 2
    return x


def _chunk_last(x, chunk):
    rows, width = x.shape
    return jnp.concatenate(
        [jnp.broadcast_to(x[n * chunk + chunk - 1:(n + 1) * chunk, :], (chunk, width))
         for n in range(rows // chunk)], axis=0)


def _gates(h, win_ref, lb, chunk):
    pq = _dot(h, win_ref[:, 0 * D_A:1 * D_A].astype(BF16))
    pf = _dot(h, win_ref[:, 1 * D_A:2 * D_A].astype(BF16))
    pv = _dot(h, win_ref[:, 2 * D_A:3 * D_A].astype(BF16))
    og = _dot(h, win_ref[:, 3 * D_A:4 * D_A].astype(BF16))
    one_m = 1.0 - lb
    f = lb + one_m * jax.nn.sigmoid(pf)
    k_in = one_m * jax.nn.sigmoid(-pf)
    b = _chunk_cumsum(jnp.log(f), chunk)
    blast = _chunk_last(b, chunk)
    q_d = pq * jnp.exp(b)
    k_d = k_in * jnp.exp(-b)
    k_w = k_in * jnp.exp(blast - b)
    return q_d, k_d, k_w, pv, og, blast


def _intra(q_d_h, k_d_h, v_h, tril):
    scores = jnp.where(tril, _dot_nt(q_d_h.astype(BF16), k_d_h.astype(BF16)), 0.0)
    return _dot(scores.astype(BF16), v_h.astype(BF16))


def _head_out(o, gh_h, og_h):
    o = o * lax.rsqrt(jnp.mean(o * o, axis=-1, keepdims=True) + EPS)
    return o * gh_h * (og_h * jax.nn.sigmoid(og_h))


def _chunk_blocks(x_h, rows, chunk):
    shift = chunk.bit_length() - 1
    rid = lax.shift_right_logical(lax.broadcasted_iota(jnp.int32, (rows, HEAD_DIM), 0), shift)
    xb = x_h.astype(BF16)
    zero = jnp.zeros_like(xb)
    return jnp.concatenate([jnp.where(rid == n, xb, zero) for n in range(rows // chunk)], axis=1)


def _conv_inputs(h, win_ref):
    bg = _dot(h, win_ref[:, 4 * D_A:4 * D_A + D_B].astype(BF16))
    cg = _dot(h, win_ref[:, 4 * D_A + D_B:4 * D_A + 2 * D_B].astype(BF16))
    vv = _dot(h, win_ref[:, 4 * D_A + 2 * D_B:4 * D_A + 3 * D_B].astype(BF16))
    return bg, cg * vv


def _merge_out(x, h, y_a_in, z, win_ref, wa_ref, wb_ref, wo_ref):
    base = 4 * D_A + 3 * D_B
    ga = _dot(h, win_ref[:, base:base + D_MODEL].astype(BF16))
    gb = _dot(h, win_ref[:, base + D_MODEL:base + 2 * D_MODEL].astype(BF16))
    y_a = _dot(y_a_in.astype(BF16), wa_ref[...].astype(BF16))
    y_b = _dot(z.astype(BF16), wb_ref[...].astype(BF16))
    merged = jax.nn.sigmoid(ga) * y_a + jax.nn.sigmoid(gb) * y_b
    return x + _dot(merged.astype(BF16), wo_ref[...].astype(BF16))


def _mix_kernel(x_ref, lbl_ref, gm_ref, win_ref, cw_ref, gh_ref, wa_ref, wb_ref, wo_ref,
                xo_ref, so_ref, co_ref, st_scr, cv_scr):
    nseq, rows, chunk = MIX_SEQS, MIX_ROWS, CHUNK
    t = pl.program_id(1)

    @pl.when(t == 0)
    def _():
        st_scr[...] = jnp.zeros_like(st_scr)
        cv_scr[...] = jnp.zeros_like(cv_scr)

    x = x_ref[...].reshape(nseq * rows, D_MODEL)
    h = _rmsnorm(x, gm_ref[...]).astype(BF16)
    lb = _lower_bound(lbl_ref[...])
    q_d, k_d, k_w, v, og, blast = _gates(h, win_ref, lb, chunk)
    tril = _causal_in_chunk(rows, chunk)
    gh = gh_ref[...]
    bg, u = _conv_inputs(h, win_ref)
    cw = cw_ref[...]
    rid = lax.broadcasted_iota(jnp.int32, (rows, D_B), 0)

    y_a_rows, conv_rows = [], []
    for s in range(nseq):
        rs = slice(s * rows, (s + 1) * rows)
        heads = []
        for hd in range(N_HEADS):
            sl = slice(hd * HEAD_DIM, (hd + 1) * HEAD_DIM)
            o_intra = _intra(q_d[rs, sl], k_d[rs, sl], v[rs, sl], tril)
            ds_t = _dot(v[rs, sl].T.astype(BF16), _chunk_blocks(k_w[rs, sl], rows, chunk))
            s_t = st_scr[s, hd]
            q_h = q_d[rs, sl].astype(BF16)
            bl_h = blast[rs, sl]
            inter = []
            for n in range(rows // chunk):
                inter.append(_dot_nt(q_h[n * chunk:(n + 1) * chunk], s_t.astype(BF16)))
                decay = jnp.exp(bl_h[n * chunk:n * chunk + 1, :])
                s_t = s_t * decay + ds_t[:, n * HEAD_DIM:(n + 1) * HEAD_DIM]
            st_scr[s, hd] = s_t
            o = o_intra + jnp.concatenate(inter, axis=0)
            heads.append(_head_out(o, gh[:, sl], og[rs, sl]))
        y_a_rows.append(jnp.concatenate(heads, axis=1))

        us = u[rs]
        prev = cv_scr[s]
        u_m1 = jnp.where(rid == 0, prev[7:8, :], pltpu.roll(us, 1, 0))
        u_m2 = jnp.where(rid == 0, prev[6:7, :], jnp.where(rid == 1, prev[7:8, :], pltpu.roll(us, 2, 0)))
        conv_rows.append(cw[0:1, :] * u_m2 + cw[1:2, :] * u_m1 + cw[2:3, :] * us)
        cv_scr[s] = us[rows - 8:rows, :]

    y_a_in = jnp.concatenate(y_a_rows, axis=0)
    z = bg * jnp.concatenate(conv_rows, axis=0)
    out = _merge_out(x, h, y_a_in, z, win_ref, wa_ref, wb_ref, wo_ref)
    xo_ref[...] = out.reshape(nseq, rows, D_MODEL)

    @pl.when(t == pl.num_programs(1) - 1)
    def _():
        for s in range(nseq):
            for hd in range(N_HEADS):
                so_ref[0, s, hd] = st_scr[s, hd].T
            co_ref[0, s] = u[(s + 1) * rows - 2:(s + 1) * rows, :]


def _mix(x, lbl, gm, win, cw, gh, wa, wb, wo):
    bsz, seq, _ = x.shape
    nseq, rows = MIX_SEQS, MIX_ROWS
    assert seq % rows == 0 and bsz % nseq == 0
    n_in = win.shape[1]
    x_spec = pl.BlockSpec((nseq, rows, D_MODEL), lambda b, t: (b, t, 0))
    return pl.pallas_call(
        _mix_kernel,
        grid=(bsz // nseq, seq // rows),
        in_specs=[x_spec, _const_spec(lbl.shape), _const_spec((1, D_MODEL)), _const_spec((D_MODEL, n_in)),
                  _const_spec((CONV_W, D_B)), _const_spec((1, D_A)), _const_spec((D_A, D_MODEL)),
                  _const_spec((D_B, D_MODEL)), _const_spec((D_MODEL, D_MODEL))],
        out_specs=[x_spec,
                   pl.BlockSpec((1, nseq, N_HEADS, HEAD_DIM, HEAD_DIM), lambda b, t: (0, b, 0, 0, 0)),
                   pl.BlockSpec((1, nseq, CONV_W - 1, D_B), lambda b, t: (0, b, 0, 0))],
        out_shape=[jax.ShapeDtypeStruct(x.shape, F32),
                   jax.ShapeDtypeStruct((1, bsz, N_HEADS, HEAD_DIM, HEAD_DIM), F32),
                   jax.ShapeDtypeStruct((1, bsz, CONV_W - 1, D_B), F32)],
        scratch_shapes=[pltpu.VMEM((nseq, N_HEADS, HEAD_DIM, HEAD_DIM), F32), pltpu.VMEM((nseq, 8, D_B), F32)],
        compiler_params=pltpu.CompilerParams(dimension_semantics=("arbitrary", "arbitrary"),
                                             vmem_limit_bytes=VMEM_LIMIT_BYTES),
        name="mix",
    )(x, lbl, gm, win, cw, gh, wa, wb, wo)


def _mix_dec_kernel(x_ref, s0_ref, c0_ref, lbl_ref, gm_ref, win_ref, cw_ref, gh_ref, wa_ref, wb_ref, wo_ref,
                    xo_ref, so_ref, co_ref, *, dec_len):
    nseq, chunk = DEC_SEQS, dec_len
    rows = nseq * chunk

    x = x_ref[...]
    h = _rmsnorm(x, gm_ref[...]).astype(BF16)
    lb = _lower_bound(lbl_ref[...])
    q_d, k_d, k_w, v, og, blast = _gates(h, win_ref, lb, chunk)
    tril = _causal_in_chunk(rows, chunk)
    gh = gh_ref[...]

    heads = []
    for hd in range(N_HEADS):
        sl = slice(hd * HEAD_DIM, (hd + 1) * HEAD_DIM)
        o_intra = _intra(q_d[:, sl], k_d[:, sl], v[:, sl], tril)
        ds = _dot(k_w[:, sl].T.astype(BF16), _chunk_blocks(v[:, sl], rows, chunk))
        decay_t = jnp.exp(blast[:, sl]).T
        q_h = q_d[:, sl].astype(BF16)
        inter = []
        for j in range(nseq):
            s0 = s0_ref[0, j, hd]
            inter.append(_dot(q_h[j * chunk:(j + 1) * chunk], s0.astype(BF16)))
            so_ref[0, j, hd] = (s0 * decay_t[:, j * chunk:j * chunk + 1]
                                + ds[:, j * HEAD_DIM:(j + 1) * HEAD_DIM])
        o = o_intra + jnp.concatenate(inter, axis=0)
        heads.append(_head_out(o, gh[:, sl], og[:, sl]))
    y_a_in = jnp.concatenate(heads, axis=1)

    bg, u = _conv_inputs(h, win_ref)
    c0 = c0_ref[0]
    tid = lax.broadcasted_iota(jnp.int32, (nseq, chunk, D_B), 1)
    u3 = u.reshape(nseq, chunk, D_B)
    r1 = pltpu.roll(u, 1, 0).reshape(nseq, chunk, D_B)
    r2 = pltpu.roll(u, 2, 0).reshape(nseq, chunk, D_B)
    u_m1 = jnp.where(tid == 0, c0[:, 1:2, :], r1)
    u_m2 = jnp.where(tid == 0, c0[:, 0:1, :], jnp.where(tid == 1, c0[:, 1:2, :], r2))
    cw = cw_ref[...]
    conv = cw[0:1, :] * u_m2 + cw[1:2, :] * u_m1 + cw[2:3, :] * u3
    co_ref[0] = u3[:, chunk - 2:chunk, :]
    z = bg * conv.reshape(rows, D_B)

    xo_ref[...] = _merge_out(x, h, y_a_in, z, win_ref, wa_ref, wb_ref, wo_ref)


def _mix_dec(x, s0, c0, lbl, gm, win, cw, gh, wa, wb, wo):
    bsz, dec_len, _ = x.shape
    assert dec_len % 8 == 0 and dec_len >= CONV_W - 1 and bsz % DEC_SEQS == 0
    rows = DEC_SEQS * dec_len
    n_in = win.shape[1]
    x_spec = pl.BlockSpec((rows, D_MODEL), lambda i: (i, 0))
    s_spec = pl.BlockSpec((1, DEC_SEQS, N_HEADS, HEAD_DIM, HEAD_DIM), lambda i: (0, i, 0, 0, 0))
    c_spec = pl.BlockSpec((1, DEC_SEQS, CONV_W - 1, D_B), lambda i: (0, i, 0, 0))
    xo, so, co = pl.pallas_call(
        functools.partial(_mix_dec_kernel, dec_len=dec_len),
        grid=(bsz // DEC_SEQS,),
        in_specs=[x_spec, s_spec, c_spec, _const_spec(lbl.shape), _const_spec((1, D_MODEL)),
                  _const_spec((D_MODEL, n_in)), _const_spec((CONV_W, D_B)), _const_spec((1, D_A)),
                  _const_spec((D_A, D_MODEL)), _const_spec((D_B, D_MODEL)), _const_spec((D_MODEL, D_MODEL))],
        out_specs=[x_spec, s_spec, c_spec],
        out_shape=[jax.ShapeDtypeStruct((bsz * dec_len, D_MODEL), F32),
                   jax.ShapeDtypeStruct(s0.shape, F32), jax.ShapeDtypeStruct(c0.shape, F32)],
        compiler_params=pltpu.CompilerParams(dimension_semantics=("arbitrary",),
                                             vmem_limit_bytes=VMEM_LIMIT_BYTES),
        name="mix_dec",
    )(x.reshape(bsz * dec_len, D_MODEL), s0, c0, lbl, gm, win, cw, gh, wa, wb, wo)
    return xo.reshape(x.shape), so, co


def kernel(x_prompt, x_sample, state_hgrn, state_conv, lower_bound_logits, g_ffn1, w1_ffn1, w3_ffn1, w2_ffn1,
           g_mix, w_in, conv_w, g_hgrn_out, w_a_out, w_b_out, w_o, g_ffn2, w1_ffn2, w3_ffn2, w2_ffn2, g_final):
    assert state_hgrn.shape[0] == 1, "single-layer trunk"
    sq = lambda w: w.reshape(w.shape[1:])
    ffn1 = (g_ffn1, sq(w1_ffn1), sq(w3_ffn1), sq(w2_ffn1))
    ffn2 = (g_ffn2, sq(w1_ffn2), sq(w3_ffn2), sq(w2_ffn2))
    gfin = g_final.reshape(1, D_MODEL)
    mixw = (lower_bound_logits, g_mix, sq(w_in), sq(conv_w), g_hgrn_out, sq(w_a_out), sq(w_b_out), sq(w_o))

    def ffn(x, weights, final_norm):
        shape = x.shape
        y = _ffn(x.reshape(-1, D_MODEL), *weights, gfin, final_norm=final_norm)
        return y.reshape(shape)

    xp = ffn(x_prompt, ffn1, False)
    xp, sh_p, sc_p = _mix(xp, *mixw)
    yp = ffn(xp, ffn2, True)

    xs = ffn(x_sample, ffn1, False)
    xs, sh_s, sc_s = _mix_dec(xs, state_hgrn, state_conv, *mixw)
    ys = ffn(xs, ffn2, True)
    return yp, ys, sh_p, sc_p, sh_s, sc_s
```

```python
import functools

import jax
import jax.numpy as jnp
from jax import lax
from jax.experimental import pallas as pl
from jax.experimental.pallas import tpu as pltpu

F32 = jnp.float32
BF16 = jnp.bfloat16

D_MODEL = 1024
D_A = 512
HEAD_DIM = 128
N_HEADS = 4
D_B = 512
CONV_W = 3
D_FF = 2816
CHUNK = 32
EPS = 1e-6

LANES = 128
VMEM_LIMIT_BYTES = 56 * 1024 * 1024

FFN_ROWS = 512
FFN_COLS = 256
MIX_ROWS = 256
MIX_SEQS = 2
DEC_SEQS = 16


def _dot(a, b):
    return jnp.dot(a, b, preferred_element_type=F32)


def _dot_nt(a, b):
    return lax.dot_general(a, b, (((1,), (1,)), ((), ())), preferred_element_type=F32)


def _rmsnorm(x, g):
    return x * lax.rsqrt(jnp.mean(x * x, axis=-1, keepdims=True) + EPS) * g


def _const_spec(shape):
    nd = len(shape)
    return pl.BlockSpec(shape, lambda *_: (0,) * nd, pipeline_mode=pl.Buffered(1))


def _ffn_kernel(x_ref, g_ref, w1_ref, w3_ref, w2_ref, gf_ref, o_ref, *, final_norm):
    x = x_ref[...]
    h = _rmsnorm(x, g_ref[...]).astype(BF16)
    acc = None
    for c in range(D_FF // FFN_COLS):
        sl = slice(c * FFN_COLS, (c + 1) * FFN_COLS)
        a = _dot(h, w1_ref[:, sl].astype(BF16))
        b = _dot(h, w3_ref[:, sl].astype(BF16))
        act = (a * jax.nn.sigmoid(a) * b).astype(BF16)
        part = _dot(act, w2_ref[sl, :].astype(BF16))
        acc = part if acc is None else acc + part
    y = x + 0.5 * acc
    if final_norm:
        y = _rmsnorm(y, gf_ref[...])
    o_ref[...] = y


def _ffn(x2d, g, w1, w3, w2, g_final, *, final_norm):
    n = x2d.shape[0]
    rows = FFN_ROWS
    assert n % rows == 0
    row_spec = pl.BlockSpec((rows, D_MODEL), lambda i: (i, 0))
    return pl.pallas_call(
        functools.partial(_ffn_kernel, final_norm=final_norm),
        grid=(n // rows,),
        in_specs=[row_spec, _const_spec((1, D_MODEL)), _const_spec((D_MODEL, D_FF)),
                  _const_spec((D_MODEL, D_FF)), _const_spec((D_FF, D_MODEL)), _const_spec((1, D_MODEL))],
        out_specs=row_spec,
        out_shape=jax.ShapeDtypeStruct((n, D_MODEL), F32),
        compiler_params=pltpu.CompilerParams(dimension_semantics=("arbitrary",),
                                             vmem_limit_bytes=VMEM_LIMIT_BYTES),
        name="ffn_final" if final_norm else "ffn",
    )(x2d, g, w1, w3, w2, g_final)


def _lower_bound(lbl):
    m = jnp.max(lbl, axis=0, keepdims=True)
    e = jnp.exp(lbl - m)
    return e[0:1, :] / jnp.sum(e, axis=0, keepdims=True)


def _causal_in_chunk(rows, chunk):
    shift = chunk.bit_length() - 1
    assert 1 << shift == chunk
    r = lax.broadcasted_iota(jnp.int32, (rows, rows), 0)
    c = lax.broadcasted_iota(jnp.int32, (rows, rows), 1)
    return (lax.shift_right_logical(r, shift) == lax.shift_right_logical(c, shift)) & (c <= r)


def _chunk_cumsum(x, chunk):
    pos = lax.broadcasted_iota(jnp.int32, x.shape, 0) & (chunk - 1)
    step = 1
    while step < chunk:
        x = x + jnp.where(pos >= step, pltpu.roll(x, step, 0), 0.0)
        step *= 2
    return x


def _chunk_last(x, chunk):
    rows, width = x.shape
    return jnp.concatenate(
        [jnp.broadcast_to(x[n * chunk + chunk - 1:(n + 1) * chunk, :], (chunk, width))
         for n in range(rows // chunk)], axis=0)


def _gates(h, win_ref, lb, chunk):
    pq = _dot(h, win_ref[:, 0 * D_A:1 * D_A].astype(BF16))
    pf = _dot(h, win_ref[:, 1 * D_A:2 * D_A].astype(BF16))
    pv = _dot(h, win_ref[:, 2 * D_A:3 * D_A].astype(BF16))
    og = _dot(h, win_ref[:, 3 * D_A:4 * D_A].astype(BF16))
    one_m = 1.0 - lb
    sig = jax.nn.sigmoid(pf)
    f = lb + one_m * sig
    k_in = one_m * (1.0 - sig)
    b = _chunk_cumsum(jnp.log(f), chunk)
    blast = _chunk_last(b, chunk)
    q_d = pq * jnp.exp(b)
    k_d = k_in * jnp.exp(-b)
    k_w = k_in * jnp.exp(blast - b)
    return q_d, k_d, k_w, pv, og, blast


def _intra(q_d_h, k_d_h, v_h, tril):
    scores = jnp.where(tril, _dot_nt(q_d_h.astype(BF16), k_d_h.astype(BF16)), 0.0)
    return _dot(scores.astype(BF16), v_h.astype(BF16))


def _head_out(o, gh_h, og_h):
    o = o * lax.rsqrt(jnp.mean(o * o, axis=-1, keepdims=True) + EPS)
    return o * gh_h * (og_h * jax.nn.sigmoid(og_h))


def _chunk_blocks(x_h, rows, chunk):
    shift = chunk.bit_length() - 1
    rid = lax.shift_right_logical(lax.broadcasted_iota(jnp.int32, (rows, HEAD_DIM), 0), shift)
    xb = x_h.astype(BF16)
    zero = jnp.zeros_like(xb)
    return jnp.concatenate([jnp.where(rid == n, xb, zero) for n in range(rows // chunk)], axis=1)


def _conv_inputs(h, win_ref):
    bg = _dot(h, win_ref[:, 4 * D_A:4 * D_A + D_B].astype(BF16))
    cg = _dot(h, win_ref[:, 4 * D_A + D_B:4 * D_A + 2 * D_B].astype(BF16))
    vv = _dot(h, win_ref[:, 4 * D_A + 2 * D_B:4 * D_A + 3 * D_B].astype(BF16))
    return bg, cg * vv


GATE_COLS = 256


def _gate_piece(h, win_ref, j):
    lo = 4 * D_A + 3 * D_B + j * GATE_COLS
    return jax.nn.sigmoid(_dot(h, win_ref[:, lo:lo + GATE_COLS].astype(BF16)))


def _merge_out(x, y_a_in, z, gates, wa_ref, wb_ref, wo_ref):
    half = D_MODEL // GATE_COLS
    y_a = _dot(y_a_in.astype(BF16), wa_ref[...].astype(BF16))
    y_b = _dot(z.astype(BF16), wb_ref[...].astype(BF16))
    merged = jnp.concatenate(gates[:half], axis=1) * y_a + jnp.concatenate(gates[half:], axis=1) * y_b
    return x + _dot(merged.astype(BF16), wo_ref[...].astype(BF16))


def _mix_kernel(x_ref, lbl_ref, gm_ref, win_ref, cw_ref, gh_ref, wa_ref, wb_ref, wo_ref,
                xo_ref, so_ref, co_ref, st_scr, cv_scr):
    nseq, rows, chunk = MIX_SEQS, MIX_ROWS, CHUNK
    t = pl.program_id(1)

    @pl.when(t == 0)
    def _():
        st_scr[...] = jnp.zeros_like(st_scr)
        cv_scr[...] = jnp.zeros_like(cv_scr)

    x = x_ref[...].reshape(nseq * rows, D_MODEL)
    h = _rmsnorm(x, gm_ref[...]).astype(BF16)
    lb = _lower_bound(lbl_ref[...])
    q_d, k_d, k_w, v, og, blast = _gates(h, win_ref, lb, chunk)
    tril = _causal_in_chunk(rows, chunk)
    gh = gh_ref[...]
    bg, u = _conv_inputs(h, win_ref)
    cw = cw_ref[...]
    rid = lax.broadcasted_iota(jnp.int32, (rows, D_B), 0)

    units = [(s, hd, slice(s * rows, (s + 1) * rows), slice(hd * HEAD_DIM, (hd + 1) * HEAD_DIM))
             for s in range(nseq) for hd in range(N_HEADS)]
    q_b = [q_d[rs, sl].astype(BF16) for _, _, rs, sl in units]
    scores = [_dot_nt(q_b[i], k_d[rs, sl].astype(BF16)) for i, (_, _, rs, sl) in enumerate(units)]
    ds_t = [_dot(v[rs, sl].T.astype(BF16), _chunk_blocks(k_w[rs, sl], rows, chunk)) for _, _, rs, sl in units]
    o_intra = [_dot(jnp.where(tril, scores[i], 0.0).astype(BF16), v[rs, sl].astype(BF16))
               for i, (_, _, rs, sl) in enumerate(units)]
    s_t = [st_scr[s, hd] for s, hd, _, _ in units]
    inter = [[] for _ in units]
    n_gate = 2 * D_MODEL // GATE_COLS
    assert n_gate == rows // chunk
    gates = []
    for n in range(rows // chunk):
        gates.append(_gate_piece(h, win_ref, n))
        for i, (_, _, rs, sl) in enumerate(units):
            inter[i].append(_dot(q_b[i][n * chunk:(n + 1) * chunk], s_t[i].T.astype(BF16)))
            decay = jnp.exp(blast[rs, sl][n * chunk:n * chunk + 1, :])
            s_t[i] = s_t[i] * decay + ds_t[i][:, n * HEAD_DIM:(n + 1) * HEAD_DIM]
    heads = []
    for i, (s, hd, rs, sl) in enumerate(units):
        st_scr[s, hd] = s_t[i]
        o = o_intra[i] + jnp.concatenate(inter[i], axis=0)
        heads.append(_head_out(o, gh[:, sl], og[rs, sl]))
    y_a_rows = [jnp.concatenate(heads[s * N_HEADS:(s + 1) * N_HEADS], axis=1) for s in range(nseq)]

    conv_rows = []
    for s in range(nseq):
        rs = slice(s * rows, (s + 1) * rows)
        us = u[rs]
        prev = cv_scr[s]
        u_m1 = jnp.where(rid == 0, prev[7:8, :], pltpu.roll(us, 1, 0))
        u_m2 = jnp.where(rid == 0, prev[6:7, :], jnp.where(rid == 1, prev[7:8, :], pltpu.roll(us, 2, 0)))
        conv_rows.append(cw[0:1, :] * u_m2 + cw[1:2, :] * u_m1 + cw[2:3, :] * us)
        cv_scr[s] = us[rows - 8:rows, :]

    y_a_in = jnp.concatenate(y_a_rows, axis=0)
    z = bg * jnp.concatenate(conv_rows, axis=0)
    out = _merge_out(x, y_a_in, z, gates, wa_ref, wb_ref, wo_ref)
    xo_ref[...] = out.reshape(nseq, rows, D_MODEL)

    @pl.when(t == pl.num_programs(1) - 1)
    def _():
        for s in range(nseq):
            for hd in range(N_HEADS):
                so_ref[0, s, hd] = st_scr[s, hd].T
            co_ref[0, s] = u[(s + 1) * rows - 2:(s + 1) * rows, :]


def _mix(x, lbl, gm, win, cw, gh, wa, wb, wo):
    bsz, seq, _ = x.shape
    nseq, rows = MIX_SEQS, MIX_ROWS
    assert seq % rows == 0 and bsz % nseq == 0
    n_in = win.shape[1]
    x_spec = pl.BlockSpec((nseq, rows, D_MODEL), lambda b, t: (b, t, 0))
    return pl.pallas_call(
        _mix_kernel,
        grid=(bsz // nseq, seq // rows),
        in_specs=[x_spec, _const_spec(lbl.shape), _const_spec((1, D_MODEL)), _const_spec((D_MODEL, n_in)),
                  _const_spec((CONV_W, D_B)), _const_spec((1, D_A)), _const_spec((D_A, D_MODEL)),
                  _const_spec((D_B, D_MODEL)), _const_spec((D_MODEL, D_MODEL))],
        out_specs=[x_spec,
                   pl.BlockSpec((1, nseq, N_HEADS, HEAD_DIM, HEAD_DIM), lambda b, t: (0, b, 0, 0, 0)),
                   pl.BlockSpec((1, nseq, CONV_W - 1, D_B), lambda b, t: (0, b, 0, 0))],
        out_shape=[jax.ShapeDtypeStruct(x.shape, F32),
                   jax.ShapeDtypeStruct((1, bsz, N_HEADS, HEAD_DIM, HEAD_DIM), F32),
                   jax.ShapeDtypeStruct((1, bsz, CONV_W - 1, D_B), F32)],
        scratch_shapes=[pltpu.VMEM((nseq, N_HEADS, HEAD_DIM, HEAD_DIM), F32), pltpu.VMEM((nseq, 8, D_B), F32)],
        compiler_params=pltpu.CompilerParams(dimension_semantics=("arbitrary", "arbitrary"),
                                             vmem_limit_bytes=VMEM_LIMIT_BYTES),
        name="mix",
    )(x, lbl, gm, win, cw, gh, wa, wb, wo)


def _mix_dec_kernel(x_ref, s0_ref, c0_ref, lbl_ref, gm_ref, win_ref, cw_ref, gh_ref, wa_ref, wb_ref, wo_ref,
                    xo_ref, so_ref, co_ref, *, dec_len):
    nseq, chunk = DEC_SEQS, dec_len
    rows = nseq * chunk

    x = x_ref[...]
    h = _rmsnorm(x, gm_ref[...]).astype(BF16)
    lb = _lower_bound(lbl_ref[...])
    q_d, k_d, k_w, v, og, blast = _gates(h, win_ref, lb, chunk)
    tril = _causal_in_chunk(rows, chunk)
    gh = gh_ref[...]

    heads = []
    for hd in range(N_HEADS):
        sl = slice(hd * HEAD_DIM, (hd + 1) * HEAD_DIM)
        o_intra = _intra(q_d[:, sl], k_d[:, sl], v[:, sl], tril)
        ds = _dot(k_w[:, sl].T.astype(BF16), _chunk_blocks(v[:, sl], rows, chunk))
        decay_t = jnp.exp(blast[:, sl]).T
        q_h = q_d[:, sl].astype(BF16)
        inter = []
        for j in range(nseq):
            s0 = s0_ref[0, j, hd]
            inter.append(_dot(q_h[j * chunk:(j + 1) * chunk], s0.astype(BF16)))
            so_ref[0, j, hd] = (s0 * decay_t[:, j * chunk:j * chunk + 1]
                                + ds[:, j * HEAD_DIM:(j + 1) * HEAD_DIM])
        o = o_intra + jnp.concatenate(inter, axis=0)
        heads.append(_head_out(o, gh[:, sl], og[:, sl]))
    y_a_in = jnp.concatenate(heads, axis=1)

    bg, u = _conv_inputs(h, win_ref)
    c0 = c0_ref[0]
    tid = lax.broadcasted_iota(jnp.int32, (nseq, chunk, D_B), 1)
    u3 = u.reshape(nseq, chunk, D_B)
    r1 = pltpu.roll(u, 1, 0).reshape(nseq, chunk, D_B)
    r2 = pltpu.roll(u, 2, 0).reshape(nseq, chunk, D_B)
    u_m1 = jnp.where(tid == 0, c0[:, 1:2, :], r1)
    u_m2 = jnp.where(tid == 0, c0[:, 0:1, :], jnp.where(tid == 1, c0[:, 1:2, :], r2))
    cw = cw_ref[...]
    conv = cw[0:1, :] * u_m2 + cw[1:2, :] * u_m1 + cw[2:3, :] * u3
    co_ref[0] = u3[:, chunk - 2:chunk, :]
    z = bg * conv.reshape(rows, D_B)

    gates = [_gate_piece(h, win_ref, j) for j in range(2 * D_MODEL // GATE_COLS)]
    xo_ref[...] = _merge_out(x, y_a_in, z, gates, wa_ref, wb_ref, wo_ref)


def _mix_dec(x, s0, c0, lbl, gm, win, cw, gh, wa, wb, wo):
    bsz, dec_len, _ = x.shape
    assert dec_len % 8 == 0 and dec_len >= CONV_W - 1 and bsz % DEC_SEQS == 0
    rows = DEC_SEQS * dec_len
    n_in = win.shape[1]
    x_spec = pl.BlockSpec((rows, D_MODEL), lambda i: (i, 0))
    s_spec = pl.BlockSpec((1, DEC_SEQS, N_HEADS, HEAD_DIM, HEAD_DIM), lambda i: (0, i, 0, 0, 0))
    c_spec = pl.BlockSpec((1, DEC_SEQS, CONV_W - 1, D_B), lambda i: (0, i, 0, 0))
    xo, so, co = pl.pallas_call(
        functools.partial(_mix_dec_kernel, dec_len=dec_len),
        grid=(bsz // DEC_SEQS,),
        in_specs=[x_spec, s_spec, c_spec, _const_spec(lbl.shape), _const_spec((1, D_MODEL)),
                  _const_spec((D_MODEL, n_in)), _const_spec((CONV_W, D_B)), _const_spec((1, D_A)),
                  _const_spec((D_A, D_MODEL)), _const_spec((D_B, D_MODEL)), _const_spec((D_MODEL, D_MODEL))],
        out_specs=[x_spec, s_spec, c_spec],
        out_shape=[jax.ShapeDtypeStruct((bsz * dec_len, D_MODEL), F32),
                   jax.ShapeDtypeStruct(s0.shape, F32), jax.ShapeDtypeStruct(c0.shape, F32)],
        compiler_params=pltpu.CompilerParams(dimension_semantics=("arbitrary",),
                                             vmem_limit_bytes=VMEM_LIMIT_BYTES),
        name="mix_dec",
    )(x.reshape(bsz * dec_len, D_MODEL), s0, c0, lbl, gm, win, cw, gh, wa, wb, wo)
    return xo.reshape(x.shape), so, co


def kernel(x_prompt, x_sample, state_hgrn, state_conv, lower_bound_logits, g_ffn1, w1_ffn1, w3_ffn1, w2_ffn1,
           g_mix, w_in, conv_w, g_hgrn_out, w_a_out, w_b_out, w_o, g_ffn2, w1_ffn2, w3_ffn2, w2_ffn2, g_final):
    assert state_hgrn.shape[0] == 1, "single-layer trunk"
    sq = lambda w: w.reshape(w.shape[1:])
    ffn1 = (g_ffn1, sq(w1_ffn1), sq(w3_ffn1), sq(w2_ffn1))
    ffn2 = (g_ffn2, sq(w1_ffn2), sq(w3_ffn2), sq(w2_ffn2))
    gfin = g_final.reshape(1, D_MODEL)
    mixw = (lower_bound_logits, g_mix, sq(w_in), sq(conv_w), g_hgrn_out, sq(w_a_out), sq(w_b_out), sq(w_o))

    def ffn(x, weights, final_norm):
        shape = x.shape
        y = _ffn(x.reshape(-1, D_MODEL), *weights, gfin, final_norm=final_norm)
        return y.reshape(shape)

    xp = ffn(x_prompt, ffn1, False)
    xp, sh_p, sc_p = _mix(xp, *mixw)
    yp = ffn(xp, ffn2, True)

    xs = ffn(x_sample, ffn1, False)
    xs, sh_s, sc_s = _mix_dec(xs, state_hgrn, state_conv, *mixw)
    ys = ffn(xs, ffn2, True)
    return yp, ys, sh_p, sc_p, sh_s, sc_s
```

```python
import functools

import jax
import jax.numpy as jnp
from jax import lax
from jax.experimental import pallas as pl
from jax.experimental.pallas import tpu as pltpu

F32 = jnp.float32
BF16 = jnp.bfloat16

D_MODEL = 1024
D_A = 512
HEAD_DIM = 128
N_HEADS = 4
D_B = 512
CONV_W = 3
D_FF = 2816
CHUNK = 32
EPS = 1e-6

LANES = 128
VMEM_LIMIT_BYTES = 56 * 1024 * 1024

FFN_ROWS = 512
FFN_COLS = 256
MIX_ROWS = 256
MIX_SEQS = 2
DEC_SEQS = 16


def _dot(a, b):
    return jnp.dot(a, b, preferred_element_type=F32)


def _dot_nt(a, b):
    return lax.dot_general(a, b, (((1,), (1,)), ((), ())), preferred_element_type=F32)


def _rmsnorm(x, g):
    return x * lax.rsqrt(jnp.mean(x * x, axis=-1, keepdims=True) + EPS) * g


def _const_spec(shape):
    nd = len(shape)
    return pl.BlockSpec(shape, lambda *_: (0,) * nd, pipeline_mode=pl.Buffered(1))


def _ffn_rows(x_ref, o_ref, g_ref, w1_ref, w3_ref, w2_ref, gf_ref, final_norm):
    x = x_ref[...]
    h = _rmsnorm(x, g_ref[...]).astype(BF16)
    acc = None
    for c in range(D_FF // FFN_COLS):
        sl = slice(c * FFN_COLS, (c + 1) * FFN_COLS)
        a = _dot(h, w1_ref[:, sl].astype(BF16))
        b = _dot(h, w3_ref[:, sl].astype(BF16))
        act = (a * jax.nn.sigmoid(a) * b).astype(BF16)
        part = _dot(act, w2_ref[sl, :].astype(BF16))
        acc = part if acc is None else acc + part
    y = x + 0.5 * acc
    if final_norm:
        y = _rmsnorm(y, gf_ref[...])
    o_ref[...] = y


def _ffn_kernel(xa_ref, xb_ref, g_ref, w1_ref, w3_ref, w2_ref, gf_ref, oa_ref, ob_ref, *, steps_a, final_norm):
    i = pl.program_id(0)

    @pl.when(i < steps_a)
    def _():
        _ffn_rows(xa_ref, oa_ref, g_ref, w1_ref, w3_ref, w2_ref, gf_ref, final_norm)

    @pl.when(i >= steps_a)
    def _():
        _ffn_rows(xb_ref, ob_ref, g_ref, w1_ref, w3_ref, w2_ref, gf_ref, final_norm)


def _ffn(xa, xb, g, w1, w3, w2, g_final, *, final_norm):
    rows = FFN_ROWS
    na, nb = xa.shape[0], xb.shape[0]
    assert na % rows == 0 and nb % rows == 0
    steps_a, steps_b = na // rows, nb // rows
    a_spec = pl.BlockSpec((rows, D_MODEL), lambda i: (jnp.minimum(i, steps_a - 1), 0))
    b_spec = pl.BlockSpec((rows, D_MODEL), lambda i: (jnp.maximum(i - steps_a, 0), 0))
    return pl.pallas_call(
        functools.partial(_ffn_kernel, steps_a=steps_a, final_norm=final_norm),
        grid=(steps_a + steps_b,),
        in_specs=[a_spec, b_spec, _const_spec((1, D_MODEL)), _const_spec((D_MODEL, D_FF)),
                  _const_spec((D_MODEL, D_FF)), _const_spec((D_FF, D_MODEL)), _const_spec((1, D_MODEL))],
        out_specs=[a_spec, b_spec],
        out_shape=[jax.ShapeDtypeStruct((na, D_MODEL), F32), jax.ShapeDtypeStruct((nb, D_MODEL), F32)],
        compiler_params=pltpu.CompilerParams(dimension_semantics=("arbitrary",),
                                             vmem_limit_bytes=VMEM_LIMIT_BYTES),
        name="ffn_final" if final_norm else "ffn",
    )(xa, xb, g, w1, w3, w2, g_final)


def _lower_bound(lbl):
    m = jnp.max(lbl, axis=0, keepdims=True)
    e = jnp.exp(lbl - m)
    return e[0:1, :] / jnp.sum(e, axis=0, keepdims=True)


def _causal_in_chunk(rows, chunk):
    shift = chunk.bit_length() - 1
    assert 1 << shift == chunk
    r = lax.broadcasted_iota(jnp.int32, (rows, rows), 0)
    c = lax.broadcasted_iota(jnp.int32, (rows, rows), 1)
    return (lax.shift_right_logical(r, shift) == lax.shift_right_logical(c, shift)) & (c <= r)


def _chunk_cumsum(x, chunk):
    pos = lax.broadcasted_iota(jnp.int32, x.shape, 0) & (chunk - 1)
    step = 1
    while step < chunk:
        x = x + jnp.where(pos >= step, pltpu.roll(x, step, 0), 0.0)
        step *= 2
    return x


def _chunk_last(x, chunk):
    rows, width = x.shape
    return jnp.concatenate(
        [jnp.broadcast_to(x[n * chunk + chunk - 1:(n + 1) * chunk, :], (chunk, width))
         for n in range(rows // chunk)], axis=0)


def _gates(h, win_ref, lb, chunk):
    pq = _dot(h, win_ref[:, 0 * D_A:1 * D_A].astype(BF16))
    pf = _dot(h, win_ref[:, 1 * D_A:2 * D_A].astype(BF16))
    pv = _dot(h, win_ref[:, 2 * D_A:3 * D_A].astype(BF16))
    og = _dot(h, win_ref[:, 3 * D_A:4 * D_A].astype(BF16))
    one_m = 1.0 - lb
    sig = jax.nn.sigmoid(pf)
    f = lb + one_m * sig
    k_in = one_m * (1.0 - sig)
    b = _chunk_cumsum(jnp.log(f), chunk)
    blast = _chunk_last(b, chunk)
    q_d = pq * jnp.exp(b)
    k_d = k_in * jnp.exp(-b)
    k_w = k_in * jnp.exp(blast - b)
    return q_d, k_d, k_w, pv, og, blast


def _intra(q_d_h, k_d_h, v_h, tril):
    scores = jnp.where(tril, _dot_nt(q_d_h.astype(BF16), k_d_h.astype(BF16)), 0.0)
    return _dot(scores.astype(BF16), v_h.astype(BF16))


def _head_out(o, gh_h, og_h):
    o = o * lax.rsqrt(jnp.mean(o * o, axis=-1, keepdims=True) + EPS)
    return o * gh_h * (og_h * jax.nn.sigmoid(og_h))


def _chunk_blocks(x_h, rows, chunk):
    shift = chunk.bit_length() - 1
    rid = lax.shift_right_logical(lax.broadcasted_iota(jnp.int32, (rows, HEAD_DIM), 0), shift)
    xb = x_h.astype(BF16)
    zero = jnp.zeros_like(xb)
    return jnp.concatenate([jnp.where(rid == n, xb, zero) for n in range(rows // chunk)], axis=1)


def _conv_inputs(h, win_ref):
    bg = _dot(h, win_ref[:, 4 * D_A:4 * D_A + D_B].astype(BF16))
    cg = _dot(h, win_ref[:, 4 * D_A + D_B:4 * D_A + 2 * D_B].astype(BF16))
    vv = _dot(h, win_ref[:, 4 * D_A + 2 * D_B:4 * D_A + 3 * D_B].astype(BF16))
    return bg, cg * vv


GATE_COLS = 256


def _gate_piece(h, win_ref, j):
    lo = 4 * D_A + 3 * D_B + j * GATE_COLS
    return jax.nn.sigmoid(_dot(h, win_ref[:, lo:lo + GATE_COLS].astype(BF16)))


def _merge_out(x, y_a_in, z, gates, wa_ref, wb_ref, wo_ref):
    half = D_MODEL // GATE_COLS
    y_a = _dot(y_a_in.astype(BF16), wa_ref[...].astype(BF16))
    y_b = _dot(z.astype(BF16), wb_ref[...].astype(BF16))
    merged = jnp.concatenate(gates[:half], axis=1) * y_a + jnp.concatenate(gates[half:], axis=1) * y_b
    return x + _dot(merged.astype(BF16), wo_ref[...].astype(BF16))


def _mix_kernel(x_ref, lbl_ref, gm_ref, win_ref, cw_ref, gh_ref, wa_ref, wb_ref, wo_ref,
                xo_ref, so_ref, co_ref, st_scr, cv_scr):
    nseq, rows, chunk = MIX_SEQS, MIX_ROWS, CHUNK
    t = pl.program_id(1)

    @pl.when(t == 0)
    def _():
        st_scr[...] = jnp.zeros_like(st_scr)
        cv_scr[...] = jnp.zeros_like(cv_scr)

    x = x_ref[...].reshape(nseq * rows, D_MODEL)
    h = _rmsnorm(x, gm_ref[...]).astype(BF16)
    lb = _lower_bound(lbl_ref[...])
    q_d, k_d, k_w, v, og, blast = _gates(h, win_ref, lb, chunk)
    tril = _causal_in_chunk(rows, chunk)
    gh = gh_ref[...]
    bg, u = _conv_inputs(h, win_ref)
    cw = cw_ref[...]
    rid = lax.broadcasted_iota(jnp.int32, (rows, D_B), 0)

    units = [(s, hd, slice(s * rows, (s + 1) * rows), slice(hd * HEAD_DIM, (hd + 1) * HEAD_DIM))
             for s in range(nseq) for hd in range(N_HEADS)]
    q_b = [q_d[rs, sl].astype(BF16) for _, _, rs, sl in units]
    scores = [_dot_nt(q_b[i], k_d[rs, sl].astype(BF16)) for i, (_, _, rs, sl) in enumerate(units)]
    ds_t = [_dot(v[rs, sl].T.astype(BF16), _chunk_blocks(k_w[rs, sl], rows, chunk)) for _, _, rs, sl in units]
    o_intra = [_dot(jnp.where(tril, scores[i], 0.0).astype(BF16), v[rs, sl].astype(BF16))
               for i, (_, _, rs, sl) in enumerate(units)]
    s_t = [st_scr[s, hd] for s, hd, _, _ in units]
    inter = [[] for _ in units]
    n_gate = 2 * D_MODEL // GATE_COLS
    assert n_gate == rows // chunk
    gates = []
    for n in range(rows // chunk):
        gates.append(_gate_piece(h, win_ref, n))
        for i, (_, _, rs, sl) in enumerate(units):
            inter[i].append(_dot(q_b[i][n * chunk:(n + 1) * chunk], s_t[i].T.astype(BF16)))
            decay = jnp.exp(blast[rs, sl][n * chunk:n * chunk + 1, :])
            s_t[i] = s_t[i] * decay + ds_t[i][:, n * HEAD_DIM:(n + 1) * HEAD_DIM]
    heads = []
    for i, (s, hd, rs, sl) in enumerate(units):
        st_scr[s, hd] = s_t[i]
        o = o_intra[i] + jnp.concatenate(inter[i], axis=0)
        heads.append(_head_out(o, gh[:, sl], og[rs, sl]))
    y_a_rows = [jnp.concatenate(heads[s * N_HEADS:(s + 1) * N_HEADS], axis=1) for s in range(nseq)]

    conv_rows = []
    for s in range(nseq):
        rs = slice(s * rows, (s + 1) * rows)
        us = u[rs]
        prev = cv_scr[s]
        u_m1 = jnp.where(rid == 0, prev[7:8, :], pltpu.roll(us, 1, 0))
        u_m2 = jnp.where(rid == 0, prev[6:7, :], jnp.where(rid == 1, prev[7:8, :], pltpu.roll(us, 2, 0)))
        conv_rows.append(cw[0:1, :] * u_m2 + cw[1:2, :] * u_m1 + cw[2:3, :] * us)
        cv_scr[s] = us[rows - 8:rows, :]

    y_a_in = jnp.concatenate(y_a_rows, axis=0)
    z = bg * jnp.concatenate(conv_rows, axis=0)
    out = _merge_out(x, y_a_in, z, gates, wa_ref, wb_ref, wo_ref)
    xo_ref[...] = out.reshape(nseq, rows, D_MODEL)

    @pl.when(t == pl.num_programs(1) - 1)
    def _():
        for s in range(nseq):
            for hd in range(N_HEADS):
                so_ref[0, s, hd] = st_scr[s, hd].T
            co_ref[0, s] = u[(s + 1) * rows - 2:(s + 1) * rows, :]


def _mix(x, lbl, gm, win, cw, gh, wa, wb, wo):
    bsz, seq, _ = x.shape
    nseq, rows = MIX_SEQS, MIX_ROWS
    assert seq % rows == 0 and bsz % nseq == 0
    n_in = win.shape[1]
    x_spec = pl.BlockSpec((nseq, rows, D_MODEL), lambda b, t: (b, t, 0))
    return pl.pallas_call(
        _mix_kernel,
        grid=(bsz // nseq, seq // rows),
        in_specs=[x_spec, _const_spec(lbl.shape), _const_spec((1, D_MODEL)), _const_spec((D_MODEL, n_in)),
                  _const_spec((CONV_W, D_B)), _const_spec((1, D_A)), _const_spec((D_A, D_MODEL)),
                  _const_spec((D_B, D_MODEL)), _const_spec((D_MODEL, D_MODEL))],
        out_specs=[x_spec,
                   pl.BlockSpec((1, nseq, N_HEADS, HEAD_DIM, HEAD_DIM), lambda b, t: (0, b, 0, 0, 0)),
                   pl.BlockSpec((1, nseq, CONV_W - 1, D_B), lambda b, t: (0, b, 0, 0))],
        out_shape=[jax.ShapeDtypeStruct(x.shape, F32),
                   jax.ShapeDtypeStruct((1, bsz, N_HEADS, HEAD_DIM, HEAD_DIM), F32),
                   jax.ShapeDtypeStruct((1, bsz, CONV_W - 1, D_B), F32)],
        scratch_shapes=[pltpu.VMEM((nseq, N_HEADS, HEAD_DIM, HEAD_DIM), F32), pltpu.VMEM((nseq, 8, D_B), F32)],
        compiler_params=pltpu.CompilerParams(dimension_semantics=("arbitrary", "arbitrary"),
                                             vmem_limit_bytes=VMEM_LIMIT_BYTES),
        name="mix",
    )(x, lbl, gm, win, cw, gh, wa, wb, wo)


def _mix_dec_kernel(x_ref, s0_ref, c0_ref, lbl_ref, gm_ref, win_ref, cw_ref, gh_ref, wa_ref, wb_ref, wo_ref,
                    xo_ref, so_ref, co_ref, *, dec_len):
    nseq, chunk = DEC_SEQS, dec_len
    rows = nseq * chunk

    x = x_ref[...]
    h = _rmsnorm(x, gm_ref[...]).astype(BF16)
    lb = _lower_bound(lbl_ref[...])
    q_d, k_d, k_w, v, og, blast = _gates(h, win_ref, lb, chunk)
    tril = _causal_in_chunk(rows, chunk)
    gh = gh_ref[...]

    heads = []
    for hd in range(N_HEADS):
        sl = slice(hd * HEAD_DIM, (hd + 1) * HEAD_DIM)
        o_intra = _intra(q_d[:, sl], k_d[:, sl], v[:, sl], tril)
        ds = _dot(k_w[:, sl].T.astype(BF16), _chunk_blocks(v[:, sl], rows, chunk))
        decay_t = jnp.exp(blast[:, sl]).T
        q_h = q_d[:, sl].astype(BF16)
        inter = []
        for j in range(nseq):
            s0 = s0_ref[0, j, hd]
            inter.append(_dot(q_h[j * chunk:(j + 1) * chunk], s0.astype(BF16)))
            so_ref[0, j, hd] = (s0 * decay_t[:, j * chunk:j * chunk + 1]
                                + ds[:, j * HEAD_DIM:(j + 1) * HEAD_DIM])
        o = o_intra + jnp.concatenate(inter, axis=0)
        heads.append(_head_out(o, gh[:, sl], og[:, sl]))
    y_a_in = jnp.concatenate(heads, axis=1)

    bg, u = _conv_inputs(h, win_ref)
    c0 = c0_ref[0]
    tid = lax.broadcasted_iota(jnp.int32, (nseq, chunk, D_B), 1)
    u3 = u.reshape(nseq, chunk, D_B)
    r1 = pltpu.roll(u, 1, 0).reshape(nseq, chunk, D_B)
    r2 = pltpu.roll(u, 2, 0).reshape(nseq, chunk, D_B)
    u_m1 = jnp.where(tid == 0, c0[:, 1:2, :], r1)
    u_m2 = jnp.where(tid == 0, c0[:, 0:1, :], jnp.where(tid == 1, c0[:, 1:2, :], r2))
    cw = cw_ref[...]
    conv = cw[0:1, :] * u_m2 + cw[1:2, :] * u_m1 + cw[2:3, :] * u3
    co_ref[0] = u3[:, chunk - 2:chunk, :]
    z = bg * conv.reshape(rows, D_B)

    gates = [_gate_piece(h, win_ref, j) for j in range(2 * D_MODEL // GATE_COLS)]
    xo_ref[...] = _merge_out(x, y_a_in, z, gates, wa_ref, wb_ref, wo_ref)


def _mix_dec(x, s0, c0, lbl, gm, win, cw, gh, wa, wb, wo):
    bsz, dec_len, _ = x.shape
    assert dec_len % 8 == 0 and dec_len >= CONV_W - 1 and bsz % DEC_SEQS == 0
    rows = DEC_SEQS * dec_len
    n_in = win.shape[1]
    x_spec = pl.BlockSpec((rows, D_MODEL), lambda i: (i, 0))
    s_spec = pl.BlockSpec((1, DEC_SEQS, N_HEADS, HEAD_DIM, HEAD_DIM), lambda i: (0, i, 0, 0, 0))
    c_spec = pl.BlockSpec((1, DEC_SEQS, CONV_W - 1, D_B), lambda i: (0, i, 0, 0))
    xo, so, co = pl.pallas_call(
        functools.partial(_mix_dec_kernel, dec_len=dec_len),
        grid=(bsz // DEC_SEQS,),
        in_specs=[x_spec, s_spec, c_spec, _const_spec(lbl.shape), _const_spec((1, D_MODEL)),
                  _const_spec((D_MODEL, n_in)), _const_spec((CONV_W, D_B)), _const_spec((1, D_A)),
                  _const_spec((D_A, D_MODEL)), _const_spec((D_B, D_MODEL)), _const_spec((D_MODEL, D_MODEL))],
        out_specs=[x_spec, s_spec, c_spec],
        out_shape=[jax.ShapeDtypeStruct((bsz * dec_len, D_MODEL), F32),
                   jax.ShapeDtypeStruct(s0.shape, F32), jax.ShapeDtypeStruct(c0.shape, F32)],
        compiler_params=pltpu.CompilerParams(dimension_semantics=("arbitrary",),
                                             vmem_limit_bytes=VMEM_LIMIT_BYTES),
        name="mix_dec",
    )(x.reshape(bsz * dec_len, D_MODEL), s0, c0, lbl, gm, win, cw, gh, wa, wb, wo)
    return xo.reshape(x.shape), so, co


def kernel(x_prompt, x_sample, state_hgrn, state_conv, lower_bound_logits, g_ffn1, w1_ffn1, w3_ffn1, w2_ffn1,
           g_mix, w_in, conv_w, g_hgrn_out, w_a_out, w_b_out, w_o, g_ffn2, w1_ffn2, w3_ffn2, w2_ffn2, g_final):
    assert state_hgrn.shape[0] == 1, "single-layer trunk"
    sq = lambda w: w.reshape(w.shape[1:])
    ffn1 = (g_ffn1, sq(w1_ffn1), sq(w3_ffn1), sq(w2_ffn1))
    ffn2 = (g_ffn2, sq(w1_ffn2), sq(w3_ffn2), sq(w2_ffn2))
    gfin = g_final.reshape(1, D_MODEL)
    mixw = (lower_bound_logits, g_mix, sq(w_in), sq(conv_w), g_hgrn_out, sq(w_a_out), sq(w_b_out), sq(w_o))

    def ffn(xa, xb, weights, final_norm):
        ya, yb = _ffn(xa.reshape(-1, D_MODEL), xb.reshape(-1, D_MODEL), *weights, gfin, final_norm=final_norm)
        return ya.reshape(xa.shape), yb.reshape(xb.shape)

    xp, xs = ffn(x_prompt, x_sample, ffn1, False)
    xp, sh_p, sc_p = _mix(xp, *mixw)
    xs, sh_s, sc_s = _mix_dec(xs, state_hgrn, state_conv, *mixw)
    yp, ys = ffn(xp, xs, ffn2, True)
    return yp, ys, sh_p, sc_p, sh_s, sc_s
```

```python
import functools

import jax
import jax.numpy as jnp
from jax import lax
from jax.experimental import pallas as pl
from jax.experimental.pallas import tpu as pltpu

F32 = jnp.float32
BF16 = jnp.bfloat16

D_MODEL = 1024
D_A = 512
HEAD_DIM = 128
N_HEADS = 4
D_B = 512
CONV_W = 3
D_FF = 2816
CHUNK = 32
EPS = 1e-6

LANES = 128
VMEM_LIMIT_BYTES = 56 * 1024 * 1024

FFN_ROWS = 512
FFN_COLS = 256
MIX_ROWS = 256
MIX_SEQS = 2
DEC_SEQS = 16


def _dot(a, b):
    return jnp.dot(a, b, preferred_element_type=F32)


def _dot_nt(a, b):
    return lax.dot_general(a, b, (((1,), (1,)), ((), ())), preferred_element_type=F32)


def _rmsnorm(x, g):
    return x * lax.rsqrt(jnp.mean(x * x, axis=-1, keepdims=True) + EPS) * g


def _norm_rows(x, g):
    r = lax.rsqrt(jnp.mean(x * x, axis=-1, keepdims=True) + EPS)
    xg = x * g
    return (xg * r).astype(BF16), xg.astype(BF16), r


def _const_spec(shape):
    nd = len(shape)
    return pl.BlockSpec(shape, lambda *_: (0,) * nd, pipeline_mode=pl.Buffered(1))


def _ffn_rows(x_ref, o_ref, g_ref, w1_ref, w3_ref, w2_ref, gf_ref, final_norm):
    x = x_ref[...]
    h, h0, r = _norm_rows(x, g_ref[...])
    acc = None
    for c in range(D_FF // FFN_COLS):
        sl = slice(c * FFN_COLS, (c + 1) * FFN_COLS)
        if c == 0:
            a = r * _dot(h0, w1_ref[:, sl].astype(BF16))
            b = r * _dot(h0, w3_ref[:, sl].astype(BF16))
        else:
            a = _dot(h, w1_ref[:, sl].astype(BF16))
            b = _dot(h, w3_ref[:, sl].astype(BF16))
        act = (a * jax.nn.sigmoid(a) * b).astype(BF16)
        part = _dot(act, w2_ref[sl, :].astype(BF16))
        acc = part if acc is None else acc + part
    y = x + 0.5 * acc
    if final_norm:
        y = _rmsnorm(y, gf_ref[...])
    o_ref[...] = y


def _ffn_kernel(xa_ref, xb_ref, g_ref, w1_ref, w3_ref, w2_ref, gf_ref, oa_ref, ob_ref, *, steps_a, final_norm):
    i = pl.program_id(0)

    @pl.when(i < steps_a)
    def _():
        _ffn_rows(xa_ref, oa_ref, g_ref, w1_ref, w3_ref, w2_ref, gf_ref, final_norm)

    @pl.when(i >= steps_a)
    def _():
        _ffn_rows(xb_ref, ob_ref, g_ref, w1_ref, w3_ref, w2_ref, gf_ref, final_norm)


def _ffn(xa, xb, g, w1, w3, w2, g_final, *, final_norm):
    rows = FFN_ROWS
    na, nb = xa.shape[0], xb.shape[0]
    assert na % rows == 0 and nb % rows == 0
    steps_a, steps_b = na // rows, nb // rows
    a_spec = pl.BlockSpec((rows, D_MODEL), lambda i: (jnp.minimum(i, steps_a - 1), 0))
    b_spec = pl.BlockSpec((rows, D_MODEL), lambda i: (jnp.maximum(i - steps_a, 0), 0))
    return pl.pallas_call(
        functools.partial(_ffn_kernel, steps_a=steps_a, final_norm=final_norm),
        grid=(steps_a + steps_b,),
        in_specs=[a_spec, b_spec, _const_spec((1, D_MODEL)), _const_spec((D_MODEL, D_FF)),
                  _const_spec((D_MODEL, D_FF)), _const_spec((D_FF, D_MODEL)), _const_spec((1, D_MODEL))],
        out_specs=[a_spec, b_spec],
        out_shape=[jax.ShapeDtypeStruct((na, D_MODEL), F32), jax.ShapeDtypeStruct((nb, D_MODEL), F32)],
        compiler_params=pltpu.CompilerParams(dimension_semantics=("arbitrary",),
                                             vmem_limit_bytes=VMEM_LIMIT_BYTES),
        name="ffn_final" if final_norm else "ffn",
    )(xa, xb, g, w1, w3, w2, g_final)


def _lower_bound(lbl):
    m = jnp.max(lbl, axis=0, keepdims=True)
    e = jnp.exp(lbl - m)
    return e[0:1, :] / jnp.sum(e, axis=0, keepdims=True)


def _causal_in_chunk(rows, chunk):
    shift = chunk.bit_length() - 1
    assert 1 << shift == chunk
    r = lax.broadcasted_iota(jnp.int32, (rows, rows), 0)
    c = lax.broadcasted_iota(jnp.int32, (rows, rows), 1)
    return (lax.shift_right_logical(r, shift) == lax.shift_right_logical(c, shift)) & (c <= r)


def _chunk_cumsum(x, chunk):
    pos = lax.broadcasted_iota(jnp.int32, x.shape, 0) & (chunk - 1)
    step = 1
    while step < chunk:
        x = x + jnp.where(pos >= step, pltpu.roll(x, step, 0), 0.0)
        step *= 2
    return x


def _chunk_last(x, chunk):
    rows, width = x.shape
    return jnp.concatenate(
        [jnp.broadcast_to(x[n * chunk + chunk - 1:(n + 1) * chunk, :], (chunk, width))
         for n in range(rows // chunk)], axis=0)


def _gates(h, h0, r, win_ref, lb, chunk):
    pf = r * _dot(h0, win_ref[:, 1 * D_A:2 * D_A].astype(BF16))
    pq = _dot(h, win_ref[:, 0 * D_A:1 * D_A].astype(BF16))
    pv = _dot(h, win_ref[:, 2 * D_A:3 * D_A].astype(BF16))
    og = _dot(h, win_ref[:, 3 * D_A:4 * D_A].astype(BF16))
    one_m = 1.0 - lb
    sig = jax.nn.sigmoid(pf)
    f = lb + one_m * sig
    k_in = one_m * (1.0 - sig)
    b = _chunk_cumsum(jnp.log(f), chunk)
    blast = _chunk_last(b, chunk)
    q_d = pq * jnp.exp(b)
    k_d = k_in * jnp.exp(-b)
    k_w = k_in * jnp.exp(blast - b)
    return q_d, k_d, k_w, pv, og, blast


def _intra(q_d_h, k_d_h, v_h, tril):
    scores = jnp.where(tril, _dot_nt(q_d_h.astype(BF16), k_d_h.astype(BF16)), 0.0)
    return _dot(scores.astype(BF16), v_h.astype(BF16))


def _head_out(o, gh_h, og_h):
    o = o * lax.rsqrt(jnp.mean(o * o, axis=-1, keepdims=True) + EPS)
    return o * gh_h * (og_h * jax.nn.sigmoid(og_h))


def _chunk_blocks(x_h, rows, chunk):
    shift = chunk.bit_length() - 1
    rid = lax.shift_right_logical(lax.broadcasted_iota(jnp.int32, (rows, HEAD_DIM), 0), shift)
    xb = x_h.astype(BF16)
    zero = jnp.zeros_like(xb)
    return jnp.concatenate([jnp.where(rid == n, xb, zero) for n in range(rows // chunk)], axis=1)


def _conv_inputs(h, win_ref):
    bg = _dot(h, win_ref[:, 4 * D_A:4 * D_A + D_B].astype(BF16))
    cg = _dot(h, win_ref[:, 4 * D_A + D_B:4 * D_A + 2 * D_B].astype(BF16))
    vv = _dot(h, win_ref[:, 4 * D_A + 2 * D_B:4 * D_A + 3 * D_B].astype(BF16))
    return bg, cg * vv


GATE_COLS = 256


def _gate_piece(h, win_ref, j):
    lo = 4 * D_A + 3 * D_B + j * GATE_COLS
    return jax.nn.sigmoid(_dot(h, win_ref[:, lo:lo + GATE_COLS].astype(BF16)))


def _merge_out(x, y_a_in, z, gates, wa_ref, wb_ref, wo_ref):
    half = D_MODEL // GATE_COLS
    y_a = _dot(y_a_in.astype(BF16), wa_ref[...].astype(BF16))
    y_b = _dot(z.astype(BF16), wb_ref[...].astype(BF16))
    merged = jnp.concatenate(gates[:half], axis=1) * y_a + jnp.concatenate(gates[half:], axis=1) * y_b
    return x + _dot(merged.astype(BF16), wo_ref[...].astype(BF16))


def _mix_kernel(x_ref, lbl_ref, gm_ref, win_ref, cw_ref, gh_ref, wa_ref, wb_ref, wo_ref,
                xo_ref, so_ref, co_ref, st_scr, cv_scr):
    nseq, rows, chunk = MIX_SEQS, MIX_ROWS, CHUNK
    t = pl.program_id(1)

    @pl.when(t == 0)
    def _():
        st_scr[...] = jnp.zeros_like(st_scr)
        cv_scr[...] = jnp.zeros_like(cv_scr)

    x = x_ref[...].reshape(nseq * rows, D_MODEL)
    h, h0, r = _norm_rows(x, gm_ref[...])
    lb = _lower_bound(lbl_ref[...])
    q_d, k_d, k_w, v, og, blast = _gates(h, h0, r, win_ref, lb, chunk)
    tril = _causal_in_chunk(rows, chunk)
    gh = gh_ref[...]
    bg, u = _conv_inputs(h, win_ref)
    cw = cw_ref[...]
    rid = lax.broadcasted_iota(jnp.int32, (rows, D_B), 0)

    units = [(s, hd, slice(s * rows, (s + 1) * rows), slice(hd * HEAD_DIM, (hd + 1) * HEAD_DIM))
             for s in range(nseq) for hd in range(N_HEADS)]
    q_b = [q_d[rs, sl].astype(BF16) for _, _, rs, sl in units]
    scores = [_dot_nt(q_b[i], k_d[rs, sl].astype(BF16)) for i, (_, _, rs, sl) in enumerate(units)]
    ds_t = [_dot(v[rs, sl].T.astype(BF16), _chunk_blocks(k_w[rs, sl], rows, chunk)) for _, _, rs, sl in units]
    o_intra = [_dot(jnp.where(tril, scores[i], 0.0).astype(BF16), v[rs, sl].astype(BF16))
               for i, (_, _, rs, sl) in enumerate(units)]
    s_t = [st_scr[s, hd] for s, hd, _, _ in units]
    inter = [[] for _ in units]
    n_gate = 2 * D_MODEL // GATE_COLS
    assert n_gate == rows // chunk
    gates = []
    for n in range(rows // chunk):
        gates.append(_gate_piece(h, win_ref, n))
        for i, (_, _, rs, sl) in enumerate(units):
            inter[i].append(_dot(q_b[i][n * chunk:(n + 1) * chunk], s_t[i].T.astype(BF16)))
            decay = jnp.exp(blast[rs, sl][n * chunk:n * chunk + 1, :])
            s_t[i] = s_t[i] * decay + ds_t[i][:, n * HEAD_DIM:(n + 1) * HEAD_DIM]
    heads = []
    for i, (s, hd, rs, sl) in enumerate(units):
        st_scr[s, hd] = s_t[i]
        o = o_intra[i] + jnp.concatenate(inter[i], axis=0)
        heads.append(_head_out(o, gh[:, sl], og[rs, sl]))
    y_a_rows = [jnp.concatenate(heads[s * N_HEADS:(s + 1) * N_HEADS], axis=1) for s in range(nseq)]

    conv_rows = []
    for s in range(nseq):
        rs = slice(s * rows, (s + 1) * rows)
        us = u[rs]
        prev = cv_scr[s]
        u_m1 = jnp.where(rid == 0, prev[7:8, :], pltpu.roll(us, 1, 0))
        u_m2 = jnp.where(rid == 0, prev[6:7, :], jnp.where(rid == 1, prev[7:8, :], pltpu.roll(us, 2, 0)))
        conv_rows.append(cw[0:1, :] * u_m2 + cw[1:2, :] * u_m1 + cw[2:3, :] * us)
        cv_scr[s] = us[rows - 8:rows, :]

    y_a_in = jnp.concatenate(y_a_rows, axis=0)
    z = bg * jnp.concatenate(conv_rows, axis=0)
    out = _merge_out(x, y_a_in, z, gates, wa_ref, wb_ref, wo_ref)
    xo_ref[...] = out.reshape(nseq, rows, D_MODEL)

    @pl.when(t == pl.num_programs(1) - 1)
    def _():
        for s in range(nseq):
            for hd in range(N_HEADS):
                so_ref[0, s, hd] = st_scr[s, hd].T
            co_ref[0, s] = u[(s + 1) * rows - 2:(s + 1) * rows, :]


def _mix(x, lbl, gm, win, cw, gh, wa, wb, wo):
    bsz, seq, _ = x.shape
    nseq, rows = MIX_SEQS, MIX_ROWS
    assert seq % rows == 0 and bsz % nseq == 0
    n_in = win.shape[1]
    x_spec = pl.BlockSpec((nseq, rows, D_MODEL), lambda b, t: (b, t, 0))
    return pl.pallas_call(
        _mix_kernel,
        grid=(bsz // nseq, seq // rows),
        in_specs=[x_spec, _const_spec(lbl.shape), _const_spec((1, D_MODEL)), _const_spec((D_MODEL, n_in)),
                  _const_spec((CONV_W, D_B)), _const_spec((1, D_A)), _const_spec((D_A, D_MODEL)),
                  _const_spec((D_B, D_MODEL)), _const_spec((D_MODEL, D_MODEL))],
        out_specs=[x_spec,
                   pl.BlockSpec((1, nseq, N_HEADS, HEAD_DIM, HEAD_DIM), lambda b, t: (0, b, 0, 0, 0)),
                   pl.BlockSpec((1, nseq, CONV_W - 1, D_B), lambda b, t: (0, b, 0, 0))],
        out_shape=[jax.ShapeDtypeStruct(x.shape, F32),
                   jax.ShapeDtypeStruct((1, bsz, N_HEADS, HEAD_DIM, HEAD_DIM), F32),
                   jax.ShapeDtypeStruct((1, bsz, CONV_W - 1, D_B), F32)],
        scratch_shapes=[pltpu.VMEM((nseq, N_HEADS, HEAD_DIM, HEAD_DIM), F32), pltpu.VMEM((nseq, 8, D_B), F32)],
        compiler_params=pltpu.CompilerParams(dimension_semantics=("arbitrary", "arbitrary"),
                                             vmem_limit_bytes=VMEM_LIMIT_BYTES),
        name="mix",
    )(x, lbl, gm, win, cw, gh, wa, wb, wo)


def _mix_dec_kernel(x_ref, s0_ref, c0_ref, lbl_ref, gm_ref, win_ref, cw_ref, gh_ref, wa_ref, wb_ref, wo_ref,
                    xo_ref, so_ref, co_ref, *, dec_len):
    nseq, chunk = DEC_SEQS, dec_len
    rows = nseq * chunk

    x = x_ref[...]
    h, h0, r = _norm_rows(x, gm_ref[...])
    lb = _lower_bound(lbl_ref[...])
    q_d, k_d, k_w, v, og, blast = _gates(h, h0, r, win_ref, lb, chunk)
    tril = _causal_in_chunk(rows, chunk)
    gh = gh_ref[...]

    heads = []
    for hd in range(N_HEADS):
        sl = slice(hd * HEAD_DIM, (hd + 1) * HEAD_DIM)
        o_intra = _intra(q_d[:, sl], k_d[:, sl], v[:, sl], tril)
        ds = _dot(k_w[:, sl].T.astype(BF16), _chunk_blocks(v[:, sl], rows, chunk))
        decay_t = jnp.exp(blast[:, sl]).T
        q_h = q_d[:, sl].astype(BF16)
        inter = []
        for j in range(nseq):
            s0 = s0_ref[0, j, hd]
            inter.append(_dot(q_h[j * chunk:(j + 1) * chunk], s0.astype(BF16)))
            so_ref[0, j, hd] = (s0 * decay_t[:, j * chunk:j * chunk + 1]
                                + ds[:, j * HEAD_DIM:(j + 1) * HEAD_DIM])
        o = o_intra + jnp.concatenate(inter, axis=0)
        heads.append(_head_out(o, gh[:, sl], og[:, sl]))
    y_a_in = jnp.concatenate(heads, axis=1)

    bg, u = _conv_inputs(h, win_ref)
    c0 = c0_ref[0]
    tid = lax.broadcasted_iota(jnp.int32, (nseq, chunk, D_B), 1)
    u3 = u.reshape(nseq, chunk, D_B)
    r1 = pltpu.roll(u, 1, 0).reshape(nseq, chunk, D_B)
    r2 = pltpu.roll(u, 2, 0).reshape(nseq, chunk, D_B)
    u_m1 = jnp.where(tid == 0, c0[:, 1:2, :], r1)
    u_m2 = jnp.where(tid == 0, c0[:, 0:1, :], jnp.where(tid == 1, c0[:, 1:2, :], r2))
    cw = cw_ref[...]
    conv = cw[0:1, :] * u_m2 + cw[1:2, :] * u_m1 + cw[2:3, :] * u3
    co_ref[0] = u3[:, chunk - 2:chunk, :]
    z = bg * conv.reshape(rows, D_B)

    gates = [_gate_piece(h, win_ref, j) for j in range(2 * D_MODEL // GATE_COLS)]
    xo_ref[...] = _merge_out(x, y_a_in, z, gates, wa_ref, wb_ref, wo_ref)


def _mix_dec(x, s0, c0, lbl, gm, win, cw, gh, wa, wb, wo):
    bsz, dec_len, _ = x.shape
    assert dec_len % 8 == 0 and dec_len >= CONV_W - 1 and bsz % DEC_SEQS == 0
    rows = DEC_SEQS * dec_len
    n_in = win.shape[1]
    x_spec = pl.BlockSpec((rows, D_MODEL), lambda i: (i, 0))
    s_spec = pl.BlockSpec((1, DEC_SEQS, N_HEADS, HEAD_DIM, HEAD_DIM), lambda i: (0, i, 0, 0, 0))
    c_spec = pl.BlockSpec((1, DEC_SEQS, CONV_W - 1, D_B), lambda i: (0, i, 0, 0))
    xo, so, co = pl.pallas_call(
        functools.partial(_mix_dec_kernel, dec_len=dec_len),
        grid=(bsz // DEC_SEQS,),
        in_specs=[x_spec, s_spec, c_spec, _const_spec(lbl.shape), _const_spec((1, D_MODEL)),
                  _const_spec((D_MODEL, n_in)), _const_spec((CONV_W, D_B)), _const_spec((1, D_A)),
                  _const_spec((D_A, D_MODEL)), _const_spec((D_B, D_MODEL)), _const_spec((D_MODEL, D_MODEL))],
        out_specs=[x_spec, s_spec, c_spec],
        out_shape=[jax.ShapeDtypeStruct((bsz * dec_len, D_MODEL), F32),
                   jax.ShapeDtypeStruct(s0.shape, F32), jax.ShapeDtypeStruct(c0.shape, F32)],
        compiler_params=pltpu.CompilerParams(dimension_semantics=("arbitrary",),
                                             vmem_limit_bytes=VMEM_LIMIT_BYTES),
        name="mix_dec",
    )(x.reshape(bsz * dec_len, D_MODEL), s0, c0, lbl, gm, win, cw, gh, wa, wb, wo)
    return xo.reshape(x.shape), so, co


def kernel(x_prompt, x_sample, state_hgrn, state_conv, lower_bound_logits, g_ffn1, w1_ffn1, w3_ffn1, w2_ffn1,
           g_mix, w_in, conv_w, g_hgrn_out, w_a_out, w_b_out, w_o, g_ffn2, w1_ffn2, w3_ffn2, w2_ffn2, g_final):
    assert state_hgrn.shape[0] == 1, "single-layer trunk"
    sq = lambda w: w.reshape(w.shape[1:])
    ffn1 = (g_ffn1, sq(w1_ffn1), sq(w3_ffn1), sq(w2_ffn1))
    ffn2 = (g_ffn2, sq(w1_ffn2), sq(w3_ffn2), sq(w2_ffn2))
    gfin = g_final.reshape(1, D_MODEL)
    mixw = (lower_bound_logits, g_mix, sq(w_in), sq(conv_w), g_hgrn_out, sq(w_a_out), sq(w_b_out), sq(w_o))

    def ffn(xa, xb, weights, final_norm):
        ya, yb = _ffn(xa.reshape(-1, D_MODEL), xb.reshape(-1, D_MODEL), *weights, gfin, final_norm=final_norm)
        return ya.reshape(xa.shape), yb.reshape(xb.shape)

    xp, xs = ffn(x_prompt, x_sample, ffn1, False)
    xp, sh_p, sc_p = _mix(xp, *mixw)
    xs, sh_s, sc_s = _mix_dec(xs, state_hgrn, state_conv, *mixw)
    yp, ys = ffn(xp, xs, ffn2, True)
    return yp, ys, sh_p, sc_p, sh_s, sc_s
```

```python
import functools

import jax
import jax.numpy as jnp
from jax import lax
from jax.experimental import pallas as pl
from jax.experimental.pallas import tpu as pltpu

F32 = jnp.float32
BF16 = jnp.bfloat16

D_MODEL = 1024
D_A = 512
HEAD_DIM = 128
N_HEADS = 4
D_B = 512
CONV_W = 3
D_FF = 2816
CHUNK = 32
EPS = 1e-6

LANES = 128
VMEM_LIMIT_BYTES = 56 * 1024 * 1024

FFN_ROWS = 512
FFN_COLS = 256
MIX_ROWS = 256
MIX_SEQS = 2
DEC_SEQS = 16


def _dot(a, b):
    return jnp.dot(a, b, preferred_element_type=F32)


def _dot_nt(a, b):
    return lax.dot_general(a, b, (((1,), (1,)), ((), ())), preferred_element_type=F32)


def _rmsnorm(x, g):
    return x * lax.rsqrt(jnp.mean(x * x, axis=-1, keepdims=True) + EPS) * g


def _norm_rows(x, g):
    r = lax.rsqrt(jnp.mean(x * x, axis=-1, keepdims=True) + EPS)
    xg = x * g
    return (xg * r).astype(BF16), xg.astype(BF16), r


def _const_spec(shape):
    nd = len(shape)
    return pl.BlockSpec(shape, lambda *_: (0,) * nd, pipeline_mode=pl.Buffered(1))


def _ffn_rows(x_ref, o_ref, g_ref, w1_ref, w3_ref, w2_ref, gf_ref, final_norm, before_chunk=None):
    x = x_ref[...]
    h, h0, r = _norm_rows(x, g_ref[...])
    acc = None
    for c in range(D_FF // FFN_COLS):
        sl = slice(c * FFN_COLS, (c + 1) * FFN_COLS)
        if before_chunk is not None:
            before_chunk(c)
        if c == 0:
            a = r * _dot(h0, w1_ref[:, sl].astype(BF16))
            b = r * _dot(h0, w3_ref[:, sl].astype(BF16))
        else:
            a = _dot(h, w1_ref[:, sl].astype(BF16))
            b = _dot(h, w3_ref[:, sl].astype(BF16))
        act = (a * jax.nn.sigmoid(a) * b).astype(BF16)
        part = _dot(act, w2_ref[sl, :].astype(BF16))
        if before_chunk is None:
            acc = part if acc is None else acc + part
        elif c == 0:
            o_ref[...] = part
        else:
            o_ref[...] += part
    y = x_ref[...] + 0.5 * (o_ref[...] if acc is None else acc)
    if final_norm:
        y = _rmsnorm(y, gf_ref[...])
    o_ref[...] = y


def _ffn_weight_copies(w1_hbm, w3_hbm, w2_hbm, w1_ref, w3_ref, w2_ref, sem, c):
    sl = pl.ds(c * FFN_COLS, FFN_COLS)
    return (pltpu.make_async_copy(w1_hbm.at[:, sl], w1_ref.at[:, sl], sem.at[c, 0]),
            pltpu.make_async_copy(w3_hbm.at[:, sl], w3_ref.at[:, sl], sem.at[c, 1]),
            pltpu.make_async_copy(w2_hbm.at[sl, :], w2_ref.at[sl, :], sem.at[c, 2]))


def _ffn_kernel(xa_ref, xb_ref, g_ref, w1_hbm, w3_hbm, w2_hbm, gf_ref, oa_ref, ob_ref,
                w1_ref, w3_ref, w2_ref, sem, *, steps_a, final_norm):
    i = pl.program_id(0)
    weights = (w1_ref, w3_ref, w2_ref)
    copies = functools.partial(_ffn_weight_copies, w1_hbm, w3_hbm, w2_hbm, w1_ref, w3_ref, w2_ref, sem)

    @pl.when(i == 0)
    def _():
        for c in range(D_FF // FFN_COLS):
            for cp in copies(c):
                cp.start()

        def arrive(c):
            for cp in copies(c):
                cp.wait()

        _ffn_rows(xa_ref, oa_ref, g_ref, *weights, gf_ref, final_norm, before_chunk=arrive)

    @pl.when((i > 0) & (i < steps_a))
    def _():
        _ffn_rows(xa_ref, oa_ref, g_ref, *weights, gf_ref, final_norm)

    @pl.when(i >= steps_a)
    def _():
        _ffn_rows(xb_ref, ob_ref, g_ref, *weights, gf_ref, final_norm)


def _ffn(xa, xb, g, w1, w3, w2, g_final, *, final_norm):
    rows = FFN_ROWS
    na, nb = xa.shape[0], xb.shape[0]
    assert na % rows == 0 and nb % rows == 0
    steps_a, steps_b = na // rows, nb // rows
    assert steps_a >= 1
    a_spec = pl.BlockSpec((rows, D_MODEL), lambda i: (jnp.minimum(i, steps_a - 1), 0))
    b_spec = pl.BlockSpec((rows, D_MODEL), lambda i: (jnp.maximum(i - steps_a, 0), 0),
                          pipeline_mode=pl.Buffered(1))
    hbm_spec = pl.BlockSpec(memory_space=pl.ANY)
    return pl.pallas_call(
        functools.partial(_ffn_kernel, steps_a=steps_a, final_norm=final_norm),
        grid=(steps_a + steps_b,),
        in_specs=[a_spec, b_spec, _const_spec((1, D_MODEL)), hbm_spec, hbm_spec, hbm_spec,
                  _const_spec((1, D_MODEL))],
        out_specs=[a_spec, b_spec],
        out_shape=[jax.ShapeDtypeStruct((na, D_MODEL), F32), jax.ShapeDtypeStruct((nb, D_MODEL), F32)],
        scratch_shapes=[pltpu.VMEM((D_MODEL, D_FF), F32), pltpu.VMEM((D_MODEL, D_FF), F32),
                        pltpu.VMEM((D_FF, D_MODEL), F32), pltpu.SemaphoreType.DMA((D_FF // FFN_COLS, 3))],
        compiler_params=pltpu.CompilerParams(dimension_semantics=("arbitrary",),
                                             vmem_limit_bytes=VMEM_LIMIT_BYTES),
        name="ffn_final" if final_norm else "ffn",
    )(xa, xb, g, w1, w3, w2, g_final)


def _lower_bound(lbl):
    m = jnp.max(lbl, axis=0, keepdims=True)
    e = jnp.exp(lbl - m)
    return e[0:1, :] / jnp.sum(e, axis=0, keepdims=True)


def _causal_in_chunk(rows, chunk):
    shift = chunk.bit_length() - 1
    assert 1 << shift == chunk
    r = lax.broadcasted_iota(jnp.int32, (rows, rows), 0)
    c = lax.broadcasted_iota(jnp.int32, (rows, rows), 1)
    return (lax.shift_right_logical(r, shift) == lax.shift_right_logical(c, shift)) & (c <= r)


def _chunk_cumsum(x, chunk):
    pos = lax.broadcasted_iota(jnp.int32, x.shape, 0) & (chunk - 1)
    step = 1
    while step < chunk:
        x = x + jnp.where(pos >= step, pltpu.roll(x, step, 0), 0.0)
        step *= 2
    return x


def _chunk_last(x, chunk):
    rows, width = x.shape
    return jnp.concatenate(
        [jnp.broadcast_to(x[n * chunk + chunk - 1:(n + 1) * chunk, :], (chunk, width))
         for n in range(rows // chunk)], axis=0)


def _gates(h, h0, r, win_ref, lb, chunk):
    pf = r * _dot(h0, win_ref[:, 1 * D_A:2 * D_A].astype(BF16))
    pq = _dot(h, win_ref[:, 0 * D_A:1 * D_A].astype(BF16))
    pv = _dot(h, win_ref[:, 2 * D_A:3 * D_A].astype(BF16))
    og = _dot(h, win_ref[:, 3 * D_A:4 * D_A].astype(BF16))
    one_m = 1.0 - lb
    sig = jax.nn.sigmoid(pf)
    f = lb + one_m * sig
    k_in = one_m * (1.0 - sig)
    b = _chunk_cumsum(jnp.log(f), chunk)
    blast = _chunk_last(b, chunk)
    q_d = pq * jnp.exp(b)
    k_d = k_in * jnp.exp(-b)
    k_w = k_in * jnp.exp(blast - b)
    return q_d, k_d, k_w, pv, og, blast


def _intra(q_d_h, k_d_h, v_h, tril):
    scores = jnp.where(tril, _dot_nt(q_d_h.astype(BF16), k_d_h.astype(BF16)), 0.0)
    return _dot(scores.astype(BF16), v_h.astype(BF16))


def _head_out(o, gh_h, og_h):
    o = o * lax.rsqrt(jnp.mean(o * o, axis=-1, keepdims=True) + EPS)
    return o * gh_h * (og_h * jax.nn.sigmoid(og_h))


def _chunk_blocks(x_h, rows, chunk):
    shift = chunk.bit_length() - 1
    rid = lax.shift_right_logical(lax.broadcasted_iota(jnp.int32, (rows, HEAD_DIM), 0), shift)
    xb = x_h.astype(BF16)
    zero = jnp.zeros_like(xb)
    return jnp.concatenate([jnp.where(rid == n, xb, zero) for n in range(rows // chunk)], axis=1)


def _conv_inputs(h, win_ref):
    bg = _dot(h, win_ref[:, 4 * D_A:4 * D_A + D_B].astype(BF16))
    cg = _dot(h, win_ref[:, 4 * D_A + D_B:4 * D_A + 2 * D_B].astype(BF16))
    vv = _dot(h, win_ref[:, 4 * D_A + 2 * D_B:4 * D_A + 3 * D_B].astype(BF16))
    return bg, cg * vv


GATE_COLS = 256


def _gate_piece(h, win_ref, j):
    lo = 4 * D_A + 3 * D_B + j * GATE_COLS
    return jax.nn.sigmoid(_dot(h, win_ref[:, lo:lo + GATE_COLS].astype(BF16)))


def _merge_out(x, y_a_in, z, gates, wa_ref, wb_ref, wo_ref):
    half = D_MODEL // GATE_COLS
    y_a = _dot(y_a_in.astype(BF16), wa_ref[...].astype(BF16))
    y_b = _dot(z.astype(BF16), wb_ref[...].astype(BF16))
    merged = jnp.concatenate(gates[:half], axis=1) * y_a + jnp.concatenate(gates[half:], axis=1) * y_b
    return x + _dot(merged.astype(BF16), wo_ref[...].astype(BF16))


def _mix_kernel(x_ref, lbl_ref, gm_ref, win_ref, cw_ref, gh_ref, wa_ref, wb_ref, wo_ref,
                xo_ref, so_ref, co_ref, st_scr, cv_scr):
    nseq, rows, chunk = MIX_SEQS, MIX_ROWS, CHUNK
    t = pl.program_id(1)

    @pl.when(t == 0)
    def _():
        st_scr[...] = jnp.zeros_like(st_scr)
        cv_scr[...] = jnp.zeros_like(cv_scr)

    x = x_ref[...].reshape(nseq * rows, D_MODEL)
    h, h0, r = _norm_rows(x, gm_ref[...])
    lb = _lower_bound(lbl_ref[...])
    q_d, k_d, k_w, v, og, blast = _gates(h, h0, r, win_ref, lb, chunk)
    tril = _causal_in_chunk(rows, chunk)
    gh = gh_ref[...]
    bg, u = _conv_inputs(h, win_ref)
    cw = cw_ref[...]
    rid = lax.broadcasted_iota(jnp.int32, (rows, D_B), 0)

    units = [(s, hd, slice(s * rows, (s + 1) * rows), slice(hd * HEAD_DIM, (hd + 1) * HEAD_DIM))
             for s in range(nseq) for hd in range(N_HEADS)]
    q_b = [q_d[rs, sl].astype(BF16) for _, _, rs, sl in units]
    scores = [_dot_nt(q_b[i], k_d[rs, sl].astype(BF16)) for i, (_, _, rs, sl) in enumerate(units)]
    ds_t = [_dot(v[rs, sl].T.astype(BF16), _chunk_blocks(k_w[rs, sl], rows, chunk)) for _, _, rs, sl in units]
    o_intra = [_dot(jnp.where(tril, scores[i], 0.0).astype(BF16), v[rs, sl].astype(BF16))
               for i, (_, _, rs, sl) in enumerate(units)]
    s_t = [st_scr[s, hd] for s, hd, _, _ in units]
    inter = [[] for _ in units]
    n_gate = 2 * D_MODEL // GATE_COLS
    assert n_gate == rows // chunk
    gates = []
    for n in range(rows // chunk):
        gates.append(_gate_piece(h, win_ref, n))
        for i, (_, _, rs, sl) in enumerate(units):
            inter[i].append(_dot(q_b[i][n * chunk:(n + 1) * chunk], s_t[i].T.astype(BF16)))
            decay = jnp.exp(blast[rs, sl][n * chunk:n * chunk + 1, :])
            s_t[i] = s_t[i] * decay + ds_t[i][:, n * HEAD_DIM:(n + 1) * HEAD_DIM]
    heads = []
    for i, (s, hd, rs, sl) in enumerate(units):
        st_scr[s, hd] = s_t[i]
        o = o_intra[i] + jnp.concatenate(inter[i], axis=0)
        heads.append(_head_out(o, gh[:, sl], og[rs, sl]))
    y_a_rows = [jnp.concatenate(heads[s * N_HEADS:(s + 1) * N_HEADS], axis=1) for s in range(nseq)]

    conv_rows = []
    for s in range(nseq):
        rs = slice(s * rows, (s + 1) * rows)
        us = u[rs]
        prev = cv_scr[s]
        u_m1 = jnp.where(rid == 0, prev[7:8, :], pltpu.roll(us, 1, 0))
        u_m2 = jnp.where(rid == 0, prev[6:7, :], jnp.where(rid == 1, prev[7:8, :], pltpu.roll(us, 2, 0)))
        conv_rows.append(cw[0:1, :] * u_m2 + cw[1:2, :] * u_m1 + cw[2:3, :] * us)
        cv_scr[s] = us[rows - 8:rows, :]

    y_a_in = jnp.concatenate(y_a_rows, axis=0)
    z = bg * jnp.concatenate(conv_rows, axis=0)
    out = _merge_out(x, y_a_in, z, gates, wa_ref, wb_ref, wo_ref)
    xo_ref[...] = out.reshape(nseq, rows, D_MODEL)

    @pl.when(t == pl.num_programs(1) - 1)
    def _():
        for s in range(nseq):
            for hd in range(N_HEADS):
                so_ref[0, s, hd] = st_scr[s, hd].T
            co_ref[0, s] = u[(s + 1) * rows - 2:(s + 1) * rows, :]


def _mix(x, lbl, gm, win, cw, gh, wa, wb, wo):
    bsz, seq, _ = x.shape
    nseq, rows = MIX_SEQS, MIX_ROWS
    assert seq % rows == 0 and bsz % nseq == 0
    n_in = win.shape[1]
    x_spec = pl.BlockSpec((nseq, rows, D_MODEL), lambda b, t: (b, t, 0))
    return pl.pallas_call(
        _mix_kernel,
        grid=(bsz // nseq, seq // rows),
        in_specs=[x_spec, _const_spec(lbl.shape), _const_spec((1, D_MODEL)), _const_spec((D_MODEL, n_in)),
                  _const_spec((CONV_W, D_B)), _const_spec((1, D_A)), _const_spec((D_A, D_MODEL)),
                  _const_spec((D_B, D_MODEL)), _const_spec((D_MODEL, D_MODEL))],
        out_specs=[x_spec,
                   pl.BlockSpec((1, nseq, N_HEADS, HEAD_DIM, HEAD_DIM), lambda b, t: (0, b, 0, 0, 0)),
                   pl.BlockSpec((1, nseq, CONV_W - 1, D_B), lambda b, t: (0, b, 0, 0))],
        out_shape=[jax.ShapeDtypeStruct(x.shape, F32),
                   jax.ShapeDtypeStruct((1, bsz, N_HEADS, HEAD_DIM, HEAD_DIM), F32),
                   jax.ShapeDtypeStruct((1, bsz, CONV_W - 1, D_B), F32)],
        scratch_shapes=[pltpu.VMEM((nseq, N_HEADS, HEAD_DIM, HEAD_DIM), F32), pltpu.VMEM((nseq, 8, D_B), F32)],
        compiler_params=pltpu.CompilerParams(dimension_semantics=("arbitrary", "arbitrary"),
                                             vmem_limit_bytes=VMEM_LIMIT_BYTES),
        name="mix",
    )(x, lbl, gm, win, cw, gh, wa, wb, wo)


def _mix_dec_kernel(x_ref, s0_ref, c0_ref, lbl_ref, gm_ref, win_ref, cw_ref, gh_ref, wa_ref, wb_ref, wo_ref,
                    xo_ref, so_ref, co_ref, *, dec_len):
    nseq, chunk = DEC_SEQS, dec_len
    rows = nseq * chunk

    x = x_ref[...]
    h, h0, r = _norm_rows(x, gm_ref[...])
    lb = _lower_bound(lbl_ref[...])
    q_d, k_d, k_w, v, og, blast = _gates(h, h0, r, win_ref, lb, chunk)
    tril = _causal_in_chunk(rows, chunk)
    gh = gh_ref[...]

    heads = []
    for hd in range(N_HEADS):
        sl = slice(hd * HEAD_DIM, (hd + 1) * HEAD_DIM)
        o_intra = _intra(q_d[:, sl], k_d[:, sl], v[:, sl], tril)
        ds = _dot(k_w[:, sl].T.astype(BF16), _chunk_blocks(v[:, sl], rows, chunk))
        decay_t = jnp.exp(blast[:, sl]).T
        q_h = q_d[:, sl].astype(BF16)
        inter = []
        for j in range(nseq):
            s0 = s0_ref[0, j, hd]
            inter.append(_dot(q_h[j * chunk:(j + 1) * chunk], s0.astype(BF16)))
            so_ref[0, j, hd] = (s0 * decay_t[:, j * chunk:j * chunk + 1]
                                + ds[:, j * HEAD_DIM:(j + 1) * HEAD_DIM])
        o = o_intra + jnp.concatenate(inter, axis=0)
        heads.append(_head_out(o, gh[:, sl], og[:, sl]))
    y_a_in = jnp.concatenate(heads, axis=1)

    bg, u = _conv_inputs(h, win_ref)
    c0 = c0_ref[0]
    tid = lax.broadcasted_iota(jnp.int32, (nseq, chunk, D_B), 1)
    u3 = u.reshape(nseq, chunk, D_B)
    r1 = pltpu.roll(u, 1, 0).reshape(nseq, chunk, D_B)
    r2 = pltpu.roll(u, 2, 0).reshape(nseq, chunk, D_B)
    u_m1 = jnp.where(tid == 0, c0[:, 1:2, :], r1)
    u_m2 = jnp.where(tid == 0, c0[:, 0:1, :], jnp.where(tid == 1, c0[:, 1:2, :], r2))
    cw = cw_ref[...]
    conv = cw[0:1, :] * u_m2 + cw[1:2, :] * u_m1 + cw[2:3, :] * u3
    co_ref[0] = u3[:, chunk - 2:chunk, :]
    z = bg * conv.reshape(rows, D_B)

    gates = [_gate_piece(h, win_ref, j) for j in range(2 * D_MODEL // GATE_COLS)]
    xo_ref[...] = _merge_out(x, y_a_in, z, gates, wa_ref, wb_ref, wo_ref)


def _mix_dec(x, s0, c0, lbl, gm, win, cw, gh, wa, wb, wo):
    bsz, dec_len, _ = x.shape
    assert dec_len % 8 == 0 and dec_len >= CONV_W - 1 and bsz % DEC_SEQS == 0
    rows = DEC_SEQS * dec_len
    n_in = win.shape[1]
    x_spec = pl.BlockSpec((rows, D_MODEL), lambda i: (i, 0))
    s_spec = pl.BlockSpec((1, DEC_SEQS, N_HEADS, HEAD_DIM, HEAD_DIM), lambda i: (0, i, 0, 0, 0))
    c_spec = pl.BlockSpec((1, DEC_SEQS, CONV_W - 1, D_B), lambda i: (0, i, 0, 0))
    xo, so, co = pl.pallas_call(
        functools.partial(_mix_dec_kernel, dec_len=dec_len),
        grid=(bsz // DEC_SEQS,),
        in_specs=[x_spec, s_spec, c_spec, _const_spec(lbl.shape), _const_spec((1, D_MODEL)),
                  _const_spec((D_MODEL, n_in)), _const_spec((CONV_W, D_B)), _const_spec((1, D_A)),
                  _const_spec((D_A, D_MODEL)), _const_spec((D_B, D_MODEL)), _const_spec((D_MODEL, D_MODEL))],
        out_specs=[x_spec, s_spec, c_spec],
        out_shape=[jax.ShapeDtypeStruct((bsz * dec_len, D_MODEL), F32),
                   jax.ShapeDtypeStruct(s0.shape, F32), jax.ShapeDtypeStruct(c0.shape, F32)],
        compiler_params=pltpu.CompilerParams(dimension_semantics=("arbitrary",),
                                             vmem_limit_bytes=VMEM_LIMIT_BYTES),
        name="mix_dec",
    )(x.reshape(bsz * dec_len, D_MODEL), s0, c0, lbl, gm, win, cw, gh, wa, wb, wo)
    return xo.reshape(x.shape), so, co


def kernel(x_prompt, x_sample, state_hgrn, state_conv, lower_bound_logits, g_ffn1, w1_ffn1, w3_ffn1, w2_ffn1,
           g_mix, w_in, conv_w, g_hgrn_out, w_a_out, w_b_out, w_o, g_ffn2, w1_ffn2, w3_ffn2, w2_ffn2, g_final):
    assert state_hgrn.shape[0] == 1, "single-layer trunk"
    sq = lambda w: w.reshape(w.shape[1:])
    ffn1 = (g_ffn1, sq(w1_ffn1), sq(w3_ffn1), sq(w2_ffn1))
    ffn2 = (g_ffn2, sq(w1_ffn2), sq(w3_ffn2), sq(w2_ffn2))
    gfin = g_final.reshape(1, D_MODEL)
    mixw = (lower_bound_logits, g_mix, sq(w_in), sq(conv_w), g_hgrn_out, sq(w_a_out), sq(w_b_out), sq(w_o))

    def ffn(xa, xb, weights, final_norm):
        ya, yb = _ffn(xa.reshape(-1, D_MODEL), xb.reshape(-1, D_MODEL), *weights, gfin, final_norm=final_norm)
        return ya.reshape(xa.shape), yb.reshape(xb.shape)

    xp, xs = ffn(x_prompt, x_sample, ffn1, False)
    xp, sh_p, sc_p = _mix(xp, *mixw)
    xs, sh_s, sc_s = _mix_dec(xs, state_hgrn, state_conv, *mixw)
    yp, ys = ffn(xp, xs, ffn2, True)
    return yp, ys, sh_p, sc_p, sh_s, sc_s
```

```python
import functools

import jax
import jax.numpy as jnp
from jax import lax
from jax.experimental import pallas as pl
from jax.experimental.pallas import tpu as pltpu

F32 = jnp.float32
BF16 = jnp.bfloat16

D_MODEL = 1024
D_A = 512
HEAD_DIM = 128
N_HEADS = 4
D_B = 512
CONV_W = 3
D_FF = 2816
CHUNK = 32
EPS = 1e-6

LANES = 128
VMEM_LIMIT_BYTES = 56 * 1024 * 1024

FFN_ROWS = 512
FFN_COLS = 256
MIX_ROWS = 256
MIX_SEQS = 2
DEC_SEQS = 16


def _dot(a, b):
    return jnp.dot(a, b, preferred_element_type=F32)


def _dot_nt(a, b):
    return lax.dot_general(a, b, (((1,), (1,)), ((), ())), preferred_element_type=F32)


def _rmsnorm(x, g):
    return x * lax.rsqrt(jnp.mean(x * x, axis=-1, keepdims=True) + EPS) * g


def _norm_rows(x, g):
    r = lax.rsqrt(jnp.mean(x * x, axis=-1, keepdims=True) + EPS)
    xg = x * g
    return (xg * r).astype(BF16), xg.astype(BF16), r


def _const_spec(shape):
    nd = len(shape)
    return pl.BlockSpec(shape, lambda *_: (0,) * nd, pipeline_mode=pl.Buffered(1))


def _ffn_rows(x_ref, o_ref, g_ref, w1_ref, w3_ref, w2_ref, gf_ref, final_norm, before_chunk=None):
    x = x_ref[...]
    h, h0, r = _norm_rows(x, g_ref[...])
    acc = None
    for c in range(D_FF // FFN_COLS):
        sl = slice(c * FFN_COLS, (c + 1) * FFN_COLS)
        if before_chunk is not None:
            before_chunk(c)
        if c == 0:
            a = r * _dot(h0, w1_ref[:, sl].astype(BF16))
            b = r * _dot(h0, w3_ref[:, sl].astype(BF16))
        else:
            a = _dot(h, w1_ref[:, sl].astype(BF16))
            b = _dot(h, w3_ref[:, sl].astype(BF16))
        act = (a * jax.nn.sigmoid(a) * b).astype(BF16)
        part = _dot(act, w2_ref[sl, :].astype(BF16))
        if before_chunk is None:
            acc = part if acc is None else acc + part
        elif c == 0:
            o_ref[...] = part
        else:
            o_ref[...] += part
    y = x_ref[...] + 0.5 * (o_ref[...] if acc is None else acc)
    if final_norm:
        y = _rmsnorm(y, gf_ref[...])
    o_ref[...] = y


def _ffn_weight_copies(w1_hbm, w3_hbm, w2_hbm, w1_ref, w3_ref, w2_ref, sem, c):
    sl = pl.ds(c * FFN_COLS, FFN_COLS)
    return (pltpu.make_async_copy(w1_hbm.at[:, sl], w1_ref.at[:, sl], sem.at[c, 0]),
            pltpu.make_async_copy(w3_hbm.at[:, sl], w3_ref.at[:, sl], sem.at[c, 1]),
            pltpu.make_async_copy(w2_hbm.at[sl, :], w2_ref.at[sl, :], sem.at[c, 2]))


def _ffn_kernel(xa_ref, xb_ref, g_ref, w1_hbm, w3_hbm, w2_hbm, gf_ref, oa_ref, ob_ref,
                w1_ref, w3_ref, w2_ref, sem, *, steps_a, final_norm):
    i = pl.program_id(0)
    weights = (w1_ref, w3_ref, w2_ref)
    copies = functools.partial(_ffn_weight_copies, w1_hbm, w3_hbm, w2_hbm, w1_ref, w3_ref, w2_ref, sem)

    @pl.when(i == 0)
    def _():
        for c in range(D_FF // FFN_COLS):
            for cp in copies(c):
                cp.start()

        def arrive(c):
            for cp in copies(c):
                cp.wait()

        _ffn_rows(xa_ref, oa_ref, g_ref, *weights, gf_ref, final_norm, before_chunk=arrive)

    @pl.when((i > 0) & (i < steps_a))
    def _():
        _ffn_rows(xa_ref, oa_ref, g_ref, *weights, gf_ref, final_norm)

    @pl.when(i >= steps_a)
    def _():
        _ffn_rows(xb_ref, ob_ref, g_ref, *weights, gf_ref, final_norm)


def _ffn(xa, xb, g, w1, w3, w2, g_final, *, final_norm):
    rows = FFN_ROWS
    na, nb = xa.shape[0], xb.shape[0]
    assert na % rows == 0 and nb % rows == 0
    steps_a, steps_b = na // rows, nb // rows
    assert steps_a >= 1
    a_spec = pl.BlockSpec((rows, D_MODEL), lambda i: (jnp.minimum(i, steps_a - 1), 0))
    b_spec = pl.BlockSpec((rows, D_MODEL), lambda i: (jnp.maximum(i - steps_a, 0), 0),
                          pipeline_mode=pl.Buffered(1))
    hbm_spec = pl.BlockSpec(memory_space=pl.ANY)
    return pl.pallas_call(
        functools.partial(_ffn_kernel, steps_a=steps_a, final_norm=final_norm),
        grid=(steps_a + steps_b,),
        in_specs=[a_spec, b_spec, _const_spec((1, D_MODEL)), hbm_spec, hbm_spec, hbm_spec,
                  _const_spec((1, D_MODEL))],
        out_specs=[a_spec, b_spec],
        out_shape=[jax.ShapeDtypeStruct((na, D_MODEL), F32), jax.ShapeDtypeStruct((nb, D_MODEL), F32)],
        scratch_shapes=[pltpu.VMEM((D_MODEL, D_FF), F32), pltpu.VMEM((D_MODEL, D_FF), F32),
                        pltpu.VMEM((D_FF, D_MODEL), F32), pltpu.SemaphoreType.DMA((D_FF // FFN_COLS, 3))],
        compiler_params=pltpu.CompilerParams(dimension_semantics=("arbitrary",),
                                             vmem_limit_bytes=VMEM_LIMIT_BYTES),
        name="ffn_final" if final_norm else "ffn",
    )(xa, xb, g, w1, w3, w2, g_final)


def _lower_bound(lbl):
    m = jnp.max(lbl, axis=0, keepdims=True)
    e = jnp.exp(lbl - m)
    return e[0:1, :] / jnp.sum(e, axis=0, keepdims=True)


def _causal_in_chunk(rows, chunk, key_major=False):
    shift = chunk.bit_length() - 1
    assert 1 << shift == chunk
    r = lax.broadcasted_iota(jnp.int32, (rows, rows), 1 if key_major else 0)
    c = lax.broadcasted_iota(jnp.int32, (rows, rows), 0 if key_major else 1)
    return (lax.shift_right_logical(r, shift) == lax.shift_right_logical(c, shift)) & (c <= r)


def _chunk_cumsum(x, chunk):
    pos = lax.broadcasted_iota(jnp.int32, x.shape, 0) & (chunk - 1)
    step = 1
    while step < chunk:
        x = x + jnp.where(pos >= step, pltpu.roll(x, step, 0), 0.0)
        step *= 2
    return x


def _chunk_last(x, chunk):
    rows, width = x.shape
    return jnp.concatenate(
        [jnp.broadcast_to(x[n * chunk + chunk - 1:(n + 1) * chunk, :], (chunk, width))
         for n in range(rows // chunk)], axis=0)


def _gates(h, h0, r, win_ref, lb, chunk):
    pf = r * _dot(h0, win_ref[:, 1 * D_A:2 * D_A].astype(BF16))
    pq = _dot(h, win_ref[:, 0 * D_A:1 * D_A].astype(BF16))
    pv = _dot(h, win_ref[:, 2 * D_A:3 * D_A].astype(BF16))
    og = _dot(h, win_ref[:, 3 * D_A:4 * D_A].astype(BF16))
    one_m = 1.0 - lb
    sig = jax.nn.sigmoid(pf)
    f = lb + one_m * sig
    k_in = one_m * (1.0 - sig)
    b = _chunk_cumsum(jnp.log(f), chunk)
    blast = _chunk_last(b, chunk)
    q_d = pq * jnp.exp(b)
    k_d = k_in * jnp.exp(-b)
    k_w = k_in * jnp.exp(blast - b)
    return q_d, k_d, k_w, pv, og, blast


def _intra(q_d_h, k_d_h, v_h, tril):
    scores = jnp.where(tril, _dot_nt(q_d_h.astype(BF16), k_d_h.astype(BF16)), 0.0)
    return _dot(scores.astype(BF16), v_h.astype(BF16))


def _head_out(o, gh_h, og_h):
    o = o * lax.rsqrt(jnp.mean(o * o, axis=-1, keepdims=True) + EPS)
    return o * gh_h * (og_h * jax.nn.sigmoid(og_h))


def _chunk_blocks(x_h, rows, chunk):
    shift = chunk.bit_length() - 1
    rid = lax.shift_right_logical(lax.broadcasted_iota(jnp.int32, (rows, HEAD_DIM), 0), shift)
    xb = x_h.astype(BF16)
    zero = jnp.zeros_like(xb)
    return jnp.concatenate([jnp.where(rid == n, xb, zero) for n in range(rows // chunk)], axis=1)


def _conv_inputs(h, win_ref):
    bg = _dot(h, win_ref[:, 4 * D_A:4 * D_A + D_B].astype(BF16))
    cg = _dot(h, win_ref[:, 4 * D_A + D_B:4 * D_A + 2 * D_B].astype(BF16))
    vv = _dot(h, win_ref[:, 4 * D_A + 2 * D_B:4 * D_A + 3 * D_B].astype(BF16))
    return bg, cg * vv


GATE_COLS = 256


def _gate_piece(h, win_ref, j):
    lo = 4 * D_A + 3 * D_B + j * GATE_COLS
    return jax.nn.sigmoid(_dot(h, win_ref[:, lo:lo + GATE_COLS].astype(BF16)))


def _merge_out(x, y_a_in, z, gates, wa_ref, wb_ref, wo_ref):
    half = D_MODEL // GATE_COLS
    y_a = _dot(y_a_in.astype(BF16), wa_ref[...].astype(BF16))
    y_b = _dot(z.astype(BF16), wb_ref[...].astype(BF16))
    merged = jnp.concatenate(gates[:half], axis=1) * y_a + jnp.concatenate(gates[half:], axis=1) * y_b
    return x + _dot(merged.astype(BF16), wo_ref[...].astype(BF16))


def _mix_kernel(x_ref, lbl_ref, gm_ref, win_ref, cw_ref, gh_ref, wa_ref, wb_ref, wo_ref,
                xo_ref, so_ref, co_ref, st_scr, cv_scr):
    nseq, rows, chunk = MIX_SEQS, MIX_ROWS, CHUNK
    t = pl.program_id(1)

    @pl.when(t == 0)
    def _():
        st_scr[...] = jnp.zeros_like(st_scr)
        cv_scr[...] = jnp.zeros_like(cv_scr)

    x = x_ref[...].reshape(nseq * rows, D_MODEL)
    h, h0, r = _norm_rows(x, gm_ref[...])
    lb = _lower_bound(lbl_ref[...])
    q_d, k_d, k_w, v, og, blast = _gates(h, h0, r, win_ref, lb, chunk)
    triu = _causal_in_chunk(rows, chunk, key_major=True)
    gh = gh_ref[...]
    bg, u = _conv_inputs(h, win_ref)
    cw = cw_ref[...]
    rid = lax.broadcasted_iota(jnp.int32, (rows, D_B), 0)

    units = [(s, hd, slice(s * rows, (s + 1) * rows), slice(hd * HEAD_DIM, (hd + 1) * HEAD_DIM))
             for s in range(nseq) for hd in range(N_HEADS)]
    q_b = [q_d[rs, sl].astype(BF16) for _, _, rs, sl in units]
    v_t = [v[rs, sl].T.astype(BF16) for _, _, rs, sl in units]
    scores_t = [_dot_nt(k_d[rs, sl].astype(BF16), q_b[i]) for i, (_, _, rs, sl) in enumerate(units)]
    ds_t = [_dot(v_t[i], _chunk_blocks(k_w[rs, sl], rows, chunk)) for i, (_, _, rs, sl) in enumerate(units)]
    o_intra = [_dot(v_t[i], jnp.where(triu, scores_t[i], 0.0).astype(BF16)).T for i in range(len(units))]
    s_t = [st_scr[s, hd] for s, hd, _, _ in units]
    inter = [[] for _ in units]
    n_gate = 2 * D_MODEL // GATE_COLS
    assert n_gate == rows // chunk
    gates = []
    for n in range(rows // chunk):
        gates.append(_gate_piece(h, win_ref, n))
        for i, (_, _, rs, sl) in enumerate(units):
            inter[i].append(_dot(q_b[i][n * chunk:(n + 1) * chunk], s_t[i].T.astype(BF16)))
            decay = jnp.exp(blast[rs, sl][n * chunk:n * chunk + 1, :])
            s_t[i] = s_t[i] * decay + ds_t[i][:, n * HEAD_DIM:(n + 1) * HEAD_DIM]
    heads = []
    for i, (s, hd, rs, sl) in enumerate(units):
        st_scr[s, hd] = s_t[i]
        o = o_intra[i] + jnp.concatenate(inter[i], axis=0)
        heads.append(_head_out(o, gh[:, sl], og[rs, sl]))
    y_a_rows = [jnp.concatenate(heads[s * N_HEADS:(s + 1) * N_HEADS], axis=1) for s in range(nseq)]

    conv_rows = []
    for s in range(nseq):
        rs = slice(s * rows, (s + 1) * rows)
        us = u[rs]
        prev = cv_scr[s]
        u_m1 = jnp.where(rid == 0, prev[7:8, :], pltpu.roll(us, 1, 0))
        u_m2 = jnp.where(rid == 0, prev[6:7, :], jnp.where(rid == 1, prev[7:8, :], pltpu.roll(us, 2, 0)))
        conv_rows.append(cw[0:1, :] * u_m2 + cw[1:2, :] * u_m1 + cw[2:3, :] * us)
        cv_scr[s] = us[rows - 8:rows, :]

    y_a_in = jnp.concatenate(y_a_rows, axis=0)
    z = bg * jnp.concatenate(conv_rows, axis=0)
    out = _merge_out(x, y_a_in, z, gates, wa_ref, wb_ref, wo_ref)
    xo_ref[...] = out.reshape(nseq, rows, D_MODEL)

    @pl.when(t == pl.num_programs(1) - 1)
    def _():
        for s in range(nseq):
            for hd in range(N_HEADS):
                so_ref[0, s, hd] = st_scr[s, hd].T
            co_ref[0, s] = u[(s + 1) * rows - 2:(s + 1) * rows, :]


def _mix(x, lbl, gm, win, cw, gh, wa, wb, wo):
    bsz, seq, _ = x.shape
    nseq, rows = MIX_SEQS, MIX_ROWS
    assert seq % rows == 0 and bsz % nseq == 0
    n_in = win.shape[1]
    x_spec = pl.BlockSpec((nseq, rows, D_MODEL), lambda b, t: (b, t, 0))
    return pl.pallas_call(
        _mix_kernel,
        grid=(bsz // nseq, seq // rows),
        in_specs=[x_spec, _const_spec(lbl.shape), _const_spec((1, D_MODEL)), _const_spec((D_MODEL, n_in)),
                  _const_spec((CONV_W, D_B)), _const_spec((1, D_A)), _const_spec((D_A, D_MODEL)),
                  _const_spec((D_B, D_MODEL)), _const_spec((D_MODEL, D_MODEL))],
        out_specs=[x_spec,
                   pl.BlockSpec((1, nseq, N_HEADS, HEAD_DIM, HEAD_DIM), lambda b, t: (0, b, 0, 0, 0)),
                   pl.BlockSpec((1, nseq, CONV_W - 1, D_B), lambda b, t: (0, b, 0, 0))],
        out_shape=[jax.ShapeDtypeStruct(x.shape, F32),
                   jax.ShapeDtypeStruct((1, bsz, N_HEADS, HEAD_DIM, HEAD_DIM), F32),
                   jax.ShapeDtypeStruct((1, bsz, CONV_W - 1, D_B), F32)],
        scratch_shapes=[pltpu.VMEM((nseq, N_HEADS, HEAD_DIM, HEAD_DIM), F32), pltpu.VMEM((nseq, 8, D_B), F32)],
        compiler_params=pltpu.CompilerParams(dimension_semantics=("arbitrary", "arbitrary"),
                                             vmem_limit_bytes=VMEM_LIMIT_BYTES),
        name="mix",
    )(x, lbl, gm, win, cw, gh, wa, wb, wo)


def _mix_dec_kernel(x_ref, s0_ref, c0_ref, lbl_ref, gm_ref, win_ref, cw_ref, gh_ref, wa_ref, wb_ref, wo_ref,
                    xo_ref, so_ref, co_ref, *, dec_len):
    nseq, chunk = DEC_SEQS, dec_len
    rows = nseq * chunk

    x = x_ref[...]
    h, h0, r = _norm_rows(x, gm_ref[...])
    lb = _lower_bound(lbl_ref[...])
    q_d, k_d, k_w, v, og, blast = _gates(h, h0, r, win_ref, lb, chunk)
    tril = _causal_in_chunk(rows, chunk)
    gh = gh_ref[...]

    heads = []
    for hd in range(N_HEADS):
        sl = slice(hd * HEAD_DIM, (hd + 1) * HEAD_DIM)
        o_intra = _intra(q_d[:, sl], k_d[:, sl], v[:, sl], tril)
        ds = _dot(k_w[:, sl].T.astype(BF16), _chunk_blocks(v[:, sl], rows, chunk))
        decay_t = jnp.exp(blast[:, sl]).T
        q_h = q_d[:, sl].astype(BF16)
        inter = []
        for j in range(nseq):
            s0 = s0_ref[0, j, hd]
            inter.append(_dot(q_h[j * chunk:(j + 1) * chunk], s0.astype(BF16)))
            so_ref[0, j, hd] = (s0 * decay_t[:, j * chunk:j * chunk + 1]
                                + ds[:, j * HEAD_DIM:(j + 1) * HEAD_DIM])
        o = o_intra + jnp.concatenate(inter, axis=0)
        heads.append(_head_out(o, gh[:, sl], og[:, sl]))
    y_a_in = jnp.concatenate(heads, axis=1)

    bg, u = _conv_inputs(h, win_ref)
    c0 = c0_ref[0]
    tid = lax.broadcasted_iota(jnp.int32, (nseq, chunk, D_B), 1)
    u3 = u.reshape(nseq, chunk, D_B)
    r1 = pltpu.roll(u, 1, 0).reshape(nseq, chunk, D_B)
    r2 = pltpu.roll(u, 2, 0).reshape(nseq, chunk, D_B)
    u_m1 = jnp.where(tid == 0, c0[:, 1:2, :], r1)
    u_m2 = jnp.where(tid == 0, c0[:, 0:1, :], jnp.where(tid == 1, c0[:, 1:2, :], r2))
    cw = cw_ref[...]
    conv = cw[0:1, :] * u_m2 + cw[1:2, :] * u_m1 + cw[2:3, :] * u3
    co_ref[0] = u3[:, chunk - 2:chunk, :]
    z = bg * conv.reshape(rows, D_B)

    gates = [_gate_piece(h, win_ref, j) for j in range(2 * D_MODEL // GATE_COLS)]
    xo_ref[...] = _merge_out(x, y_a_in, z, gates, wa_ref, wb_ref, wo_ref)


def _mix_dec(x, s0, c0, lbl, gm, win, cw, gh, wa, wb, wo):
    bsz, dec_len, _ = x.shape
    assert dec_len % 8 == 0 and dec_len >= CONV_W - 1 and bsz % DEC_SEQS == 0
    rows = DEC_SEQS * dec_len
    n_in = win.shape[1]
    x_spec = pl.BlockSpec((rows, D_MODEL), lambda i: (i, 0))
    s_spec = pl.BlockSpec((1, DEC_SEQS, N_HEADS, HEAD_DIM, HEAD_DIM), lambda i: (0, i, 0, 0, 0))
    c_spec = pl.BlockSpec((1, DEC_SEQS, CONV_W - 1, D_B), lambda i: (0, i, 0, 0))
    xo, so, co = pl.pallas_call(
        functools.partial(_mix_dec_kernel, dec_len=dec_len),
        grid=(bsz // DEC_SEQS,),
        in_specs=[x_spec, s_spec, c_spec, _const_spec(lbl.shape), _const_spec((1, D_MODEL)),
                  _const_spec((D_MODEL, n_in)), _const_spec((CONV_W, D_B)), _const_spec((1, D_A)),
                  _const_spec((D_A, D_MODEL)), _const_spec((D_B, D_MODEL)), _const_spec((D_MODEL, D_MODEL))],
        out_specs=[x_spec, s_spec, c_spec],
        out_shape=[jax.ShapeDtypeStruct((bsz * dec_len, D_MODEL), F32),
                   jax.ShapeDtypeStruct(s0.shape, F32), jax.ShapeDtypeStruct(c0.shape, F32)],
        compiler_params=pltpu.CompilerParams(dimension_semantics=("arbitrary",),
                                             vmem_limit_bytes=VMEM_LIMIT_BYTES),
        name="mix_dec",
    )(x.reshape(bsz * dec_len, D_MODEL), s0, c0, lbl, gm, win, cw, gh, wa, wb, wo)
    return xo.reshape(x.shape), so, co


def kernel(x_prompt, x_sample, state_hgrn, state_conv, lower_bound_logits, g_ffn1, w1_ffn1, w3_ffn1, w2_ffn1,
           g_mix, w_in, conv_w, g_hgrn_out, w_a_out, w_b_out, w_o, g_ffn2, w1_ffn2, w3_ffn2, w2_ffn2, g_final):
    assert state_hgrn.shape[0] == 1, "single-layer trunk"
    sq = lambda w: w.reshape(w.shape[1:])
    ffn1 = (g_ffn1, sq(w1_ffn1), sq(w3_ffn1), sq(w2_ffn1))
    ffn2 = (g_ffn2, sq(w1_ffn2), sq(w3_ffn2), sq(w2_ffn2))
    gfin = g_final.reshape(1, D_MODEL)
    mixw = (lower_bound_logits, g_mix, sq(w_in), sq(conv_w), g_hgrn_out, sq(w_a_out), sq(w_b_out), sq(w_o))

    def ffn(xa, xb, weights, final_norm):
        ya, yb = _ffn(xa.reshape(-1, D_MODEL), xb.reshape(-1, D_MODEL), *weights, gfin, final_norm=final_norm)
        return ya.reshape(xa.shape), yb.reshape(xb.shape)

    xp, xs = ffn(x_prompt, x_sample, ffn1, False)
    xp, sh_p, sc_p = _mix(xp, *mixw)
    xs, sh_s, sc_s = _mix_dec(xs, state_hgrn, state_conv, *mixw)
    yp, ys = ffn(xp, xs, ffn2, True)
    return yp, ys, sh_p, sc_p, sh_s, sc_s
```

```python
import functools

import jax
import jax.numpy as jnp
from jax import lax
from jax.experimental import pallas as pl
from jax.experimental.pallas import tpu as pltpu

F32 = jnp.float32
BF16 = jnp.bfloat16

D_MODEL = 1024
D_A = 512
HEAD_DIM = 128
N_HEADS = 4
D_B = 512
CONV_W = 3
D_FF = 2816
CHUNK = 32
EPS = 1e-6

VMEM_LIMIT_BYTES = 56 * 1024 * 1024

FFN_ROWS = 512
FFN_COLS = 256
MIX_ROWS = 256
MIX_SEQS = 2
DEC_SEQS = 4
GATE_COLS = 256


def _dot(a, b):
    return jnp.dot(a, b, preferred_element_type=F32)


def _dot_nt(a, b):
    return lax.dot_general(a, b, (((1,), (1,)), ((), ())), preferred_element_type=F32)


def _rmsnorm(x, g):
    return x * lax.rsqrt(jnp.mean(x * x, axis=-1, keepdims=True) + EPS) * g


def _norm_rows(x, g):
    r = lax.rsqrt(jnp.mean(x * x, axis=-1, keepdims=True) + EPS)
    xg = x * g
    return (xg * r).astype(BF16), xg.astype(BF16), r


def _const_spec(shape):
    nd = len(shape)
    return pl.BlockSpec(shape, lambda *_: (0,) * nd, pipeline_mode=pl.Buffered(1))


def _ffn_rows(x_ref, o_ref, g_ref, w1_ref, w3_ref, w2_ref, gf_ref, final_norm, before_chunk=None):
    x = x_ref[...]
    h, h0, r = _norm_rows(x, g_ref[...])
    acc = None
    for c in range(D_FF // FFN_COLS):
        sl = slice(c * FFN_COLS, (c + 1) * FFN_COLS)
        if before_chunk is not None:
            before_chunk(c)
        if c == 0:
            a = r * _dot(h0, w1_ref[:, sl].astype(BF16))
            b = r * _dot(h0, w3_ref[:, sl].astype(BF16))
        else:
            a = _dot(h, w1_ref[:, sl].astype(BF16))
            b = _dot(h, w3_ref[:, sl].astype(BF16))
        act = (a * jax.nn.sigmoid(a) * b).astype(BF16)
        part = _dot(act, w2_ref[sl, :].astype(BF16))
        if before_chunk is None:
            acc = part if acc is None else acc + part
        elif c == 0:
            o_ref[...] = part
        else:
            o_ref[...] += part
    y = x_ref[...] + 0.5 * (o_ref[...] if acc is None else acc)
    if final_norm:
        y = _rmsnorm(y, gf_ref[...])
    o_ref[...] = y


def _ffn_weight_copies(w1_hbm, w3_hbm, w2_hbm, w1_ref, w3_ref, w2_ref, sem, c):
    sl = pl.ds(c * FFN_COLS, FFN_COLS)
    return (pltpu.make_async_copy(w1_hbm.at[:, sl], w1_ref.at[:, sl], sem.at[c, 0]),
            pltpu.make_async_copy(w3_hbm.at[:, sl], w3_ref.at[:, sl], sem.at[c, 1]),
            pltpu.make_async_copy(w2_hbm.at[sl, :], w2_ref.at[sl, :], sem.at[c, 2]))


def _ffn_kernel(xa_ref, xb_ref, g_ref, w1_hbm, w3_hbm, w2_hbm, gf_ref, oa_ref, ob_ref,
                w1_ref, w3_ref, w2_ref, sem, *, steps_a, final_norm):
    i = pl.program_id(0)
    weights = (w1_ref, w3_ref, w2_ref)
    copies = functools.partial(_ffn_weight_copies, w1_hbm, w3_hbm, w2_hbm, w1_ref, w3_ref, w2_ref, sem)

    @pl.when(i == 0)
    def _():
        for c in range(D_FF // FFN_COLS):
            for cp in copies(c):
                cp.start()

        def arrive(c):
            for cp in copies(c):
                cp.wait()

        _ffn_rows(xa_ref, oa_ref, g_ref, *weights, gf_ref, final_norm, before_chunk=arrive)

    @pl.when((i > 0) & (i < steps_a))
    def _():
        _ffn_rows(xa_ref, oa_ref, g_ref, *weights, gf_ref, final_norm)

    @pl.when(i >= steps_a)
    def _():
        _ffn_rows(xb_ref, ob_ref, g_ref, *weights, gf_ref, final_norm)


def _ffn(xa, xb, g, w1, w3, w2, g_final, *, final_norm):
    rows = FFN_ROWS
    na, nb = xa.shape[0], xb.shape[0]
    assert na % rows == 0 and nb % rows == 0
    steps_a, steps_b = na // rows, nb // rows
    assert steps_a >= 1
    a_spec = pl.BlockSpec((rows, D_MODEL), lambda i: (jnp.minimum(i, steps_a - 1), 0))
    b_spec = pl.BlockSpec((rows, D_MODEL), lambda i: (jnp.maximum(i - steps_a, 0), 0),
                          pipeline_mode=pl.Buffered(1))
    hbm_spec = pl.BlockSpec(memory_space=pl.ANY)
    return pl.pallas_call(
        functools.partial(_ffn_kernel, steps_a=steps_a, final_norm=final_norm),
        grid=(steps_a + steps_b,),
        in_specs=[a_spec, b_spec, _const_spec((1, D_MODEL)), hbm_spec, hbm_spec, hbm_spec,
                  _const_spec((1, D_MODEL))],
        out_specs=[a_spec, b_spec],
        out_shape=[jax.ShapeDtypeStruct((na, D_MODEL), F32), jax.ShapeDtypeStruct((nb, D_MODEL), F32)],
        scratch_shapes=[pltpu.VMEM((D_MODEL, D_FF), F32), pltpu.VMEM((D_MODEL, D_FF), F32),
                        pltpu.VMEM((D_FF, D_MODEL), F32), pltpu.SemaphoreType.DMA((D_FF // FFN_COLS, 3))],
        compiler_params=pltpu.CompilerParams(dimension_semantics=("arbitrary",),
                                             vmem_limit_bytes=VMEM_LIMIT_BYTES),
        name="ffn_final" if final_norm else "ffn",
    )(xa, xb, g, w1, w3, w2, g_final)


def _lower_bound(lbl):
    m = jnp.max(lbl, axis=0, keepdims=True)
    e = jnp.exp(lbl - m)
    return e[0:1, :] / jnp.sum(e, axis=0, keepdims=True)


def _causal_in_chunk(rows, chunk, key_major=False):
    shift = chunk.bit_length() - 1
    assert 1 << shift == chunk
    r = lax.broadcasted_iota(jnp.int32, (rows, rows), 1 if key_major else 0)
    c = lax.broadcasted_iota(jnp.int32, (rows, rows), 0 if key_major else 1)
    return (lax.shift_right_logical(r, shift) == lax.shift_right_logical(c, shift)) & (c <= r)


def _chunk_cumsum(x, chunk):
    pos = lax.broadcasted_iota(jnp.int32, x.shape, 0) & (chunk - 1)
    step = 1
    while step < chunk:
        x = x + jnp.where(pos >= step, pltpu.roll(x, step, 0), 0.0)
        step *= 2
    return x


def _chunk_last(x, chunk):
    rows, width = x.shape
    return jnp.concatenate(
        [jnp.broadcast_to(x[n * chunk + chunk - 1:(n + 1) * chunk, :], (chunk, width))
         for n in range(rows // chunk)], axis=0)


def _gate_products(h, h0, r, win_ref):
    pf = r * _dot(h0, win_ref[:, 1 * D_A:2 * D_A].astype(BF16))
    pq = _dot(h, win_ref[:, 0 * D_A:1 * D_A].astype(BF16))
    pv = _dot(h, win_ref[:, 2 * D_A:3 * D_A].astype(BF16))
    og = _dot(h, win_ref[:, 3 * D_A:4 * D_A].astype(BF16))
    return pq, pf, pv, og


def _decay_chain(pq, pf, lb, chunk):
    one_m = 1.0 - lb
    sig = jax.nn.sigmoid(pf)
    f = lb + one_m * sig
    k_in = one_m * (1.0 - sig)
    b = _chunk_cumsum(jnp.log(f), chunk)
    blast = _chunk_last(b, chunk)
    q_d = pq * jnp.exp(b)
    k_d = k_in * jnp.exp(-b)
    k_w = k_in * jnp.exp(blast - b)
    return q_d, k_d, k_w, blast


def _intra(q_b, k_d_h, v_h, tril):
    scores = jnp.where(tril, _dot_nt(q_b, k_d_h.astype(BF16)), 0.0)
    return _dot(scores.astype(BF16), v_h.astype(BF16))


def _head_out(o, gh_h, og_h):
    o = o * lax.rsqrt(jnp.mean(o * o, axis=-1, keepdims=True) + EPS)
    return o * gh_h * (og_h * jax.nn.sigmoid(og_h))


def _chunk_blocks(x_h, rows, chunk):
    shift = chunk.bit_length() - 1
    rid = lax.shift_right_logical(lax.broadcasted_iota(jnp.int32, (rows, HEAD_DIM), 0), shift)
    xb = x_h.astype(BF16)
    zero = jnp.zeros_like(xb)
    return jnp.concatenate([jnp.where(rid == n, xb, zero) for n in range(rows // chunk)], axis=1)


def _conv_inputs(h, win_ref):
    bg = _dot(h, win_ref[:, 4 * D_A:4 * D_A + D_B].astype(BF16))
    cg = _dot(h, win_ref[:, 4 * D_A + D_B:4 * D_A + 2 * D_B].astype(BF16))
    vv = _dot(h, win_ref[:, 4 * D_A + 2 * D_B:4 * D_A + 3 * D_B].astype(BF16))
    return bg, cg * vv


def _gate_piece(h, win_ref, j):
    lo = 4 * D_A + 3 * D_B + j * GATE_COLS
    return jax.nn.sigmoid(_dot(h, win_ref[:, lo:lo + GATE_COLS].astype(BF16)))


def _merge_out(x, y_a_in, z, gates, wa_ref, wb_ref, wo_ref):
    half = D_MODEL // GATE_COLS
    y_a = _dot(y_a_in.astype(BF16), wa_ref[...].astype(BF16))
    y_b = _dot(z.astype(BF16), wb_ref[...].astype(BF16))
    merged = jnp.concatenate(gates[:half], axis=1) * y_a + jnp.concatenate(gates[half:], axis=1) * y_b
    return x + _dot(merged.astype(BF16), wo_ref[...].astype(BF16))


def _decode_recurrence(q_d, k_d, k_w, v, blast, s0_ref, so_ref, nseq, chunk):
    rows = nseq * chunk
    tril = _causal_in_chunk(rows, chunk)
    outs = []
    for hd in range(N_HEADS):
        sl = slice(hd * HEAD_DIM, (hd + 1) * HEAD_DIM)
        q_b = q_d[:, sl].astype(BF16)
        o_intra = _intra(q_b, k_d[:, sl], v[:, sl], tril)
        ds = _dot(k_w[:, sl].T.astype(BF16), _chunk_blocks(v[:, sl], rows, chunk))
        decay_t = jnp.exp(blast[:, sl]).T
        inter = []
        for j in range(nseq):
            s0 = s0_ref[0, j, hd]
            inter.append(_dot(q_b[j * chunk:(j + 1) * chunk], s0.astype(BF16)))
            so_ref[0, j, hd] = (s0 * decay_t[:, j * chunk:j * chunk + 1]
                                + ds[:, j * HEAD_DIM:(j + 1) * HEAD_DIM])
        outs.append(o_intra + jnp.concatenate(inter, axis=0))
    return outs


def _decode_conv(u, c0, cw, nseq, chunk):
    tid = lax.broadcasted_iota(jnp.int32, (nseq, chunk, D_B), 1)
    u3 = u.reshape(nseq, chunk, D_B)
    r1 = pltpu.roll(u, 1, 0).reshape(nseq, chunk, D_B)
    r2 = pltpu.roll(u, 2, 0).reshape(nseq, chunk, D_B)
    u_m1 = jnp.where(tid == 0, c0[:, 1:2, :], r1)
    u_m2 = jnp.where(tid == 0, c0[:, 0:1, :], jnp.where(tid == 1, c0[:, 1:2, :], r2))
    conv = cw[0:1, :] * u_m2 + cw[1:2, :] * u_m1 + cw[2:3, :] * u3
    return conv.reshape(nseq * chunk, D_B), u3[:, chunk - 2:chunk, :]


def _mix_kernel(x_ref, xd_ref, s0_ref, c0_ref, lbl_ref, gm_ref, win_ref, cw_ref, gh_ref, wa_ref, wb_ref, wo_ref,
                xo_ref, so_ref, co_ref, xod_ref, sod_ref, cod_ref, st_scr, cv_scr, *, dec_len):
    nseq, rows, chunk = MIX_SEQS, MIX_ROWS, CHUNK
    prow = nseq * rows
    t = pl.program_id(1)

    @pl.when(t == 0)
    def _():
        st_scr[...] = jnp.zeros_like(st_scr)
        cv_scr[...] = jnp.zeros_like(cv_scr)

    x = jnp.concatenate([x_ref[...].reshape(prow, D_MODEL), xd_ref[...]], axis=0)
    h, h0, r = _norm_rows(x, gm_ref[...])
    lb = _lower_bound(lbl_ref[...])
    pq, pf, v, og = _gate_products(h, h0, r, win_ref)
    q_d, k_d, k_w, blast = _decay_chain(pq[:prow], pf[:prow], lb, chunk)
    qd_d, kd_d, kw_d, blast_d = _decay_chain(pq[prow:], pf[prow:], lb, dec_len)
    triu = _causal_in_chunk(rows, chunk, key_major=True)
    gh = gh_ref[...]
    bg, u = _conv_inputs(h, win_ref)
    cw = cw_ref[...]
    rid = lax.broadcasted_iota(jnp.int32, (rows, D_B), 0)

    units = [(s, hd, slice(s * rows, (s + 1) * rows), slice(hd * HEAD_DIM, (hd + 1) * HEAD_DIM))
             for s in range(nseq) for hd in range(N_HEADS)]
    q_b = [q_d[rs, sl].astype(BF16) for _, _, rs, sl in units]
    v_t = [v[rs, sl].T.astype(BF16) for _, _, rs, sl in units]
    scores_t = [_dot_nt(k_d[rs, sl].astype(BF16), q_b[i]) for i, (_, _, rs, sl) in enumerate(units)]
    ds_t = [_dot(v_t[i], _chunk_blocks(k_w[rs, sl], rows, chunk)) for i, (_, _, rs, sl) in enumerate(units)]
    o_intra = [_dot(v_t[i], jnp.where(triu, scores_t[i], 0.0).astype(BF16)).T for i in range(len(units))]
    s_t = [st_scr[s, hd] for s, hd, _, _ in units]
    inter = [[] for _ in units]
    assert 2 * D_MODEL // GATE_COLS == rows // chunk
    gates = []
    for n in range(rows // chunk):
        gates.append(_gate_piece(h, win_ref, n))
        for i, (_, _, rs, sl) in enumerate(units):
            inter[i].append(_dot(q_b[i][n * chunk:(n + 1) * chunk], s_t[i].T.astype(BF16)))
            decay = jnp.exp(blast[rs, sl][n * chunk:n * chunk + 1, :])
            s_t[i] = s_t[i] * decay + ds_t[i][:, n * HEAD_DIM:(n + 1) * HEAD_DIM]
    o_dec = _decode_recurrence(qd_d, kd_d, kw_d, v[prow:], blast_d, s0_ref, sod_ref, DEC_SEQS, dec_len)
    heads = []
    for i, (s, hd, rs, sl) in enumerate(units):
        st_scr[s, hd] = s_t[i]
        o = o_intra[i] + jnp.concatenate(inter[i], axis=0)
        heads.append(_head_out(o, gh[:, sl], og[rs, sl]))
    y_a_rows = [jnp.concatenate(heads[s * N_HEADS:(s + 1) * N_HEADS], axis=1) for s in range(nseq)]
    y_a_rows.append(jnp.concatenate(
        [_head_out(o_dec[hd], gh[:, hd * HEAD_DIM:(hd + 1) * HEAD_DIM], og[prow:, hd * HEAD_DIM:(hd + 1) * HEAD_DIM])
         for hd in range(N_HEADS)], axis=1))

    conv_rows = []
    for s in range(nseq):
        rs = slice(s * rows, (s + 1) * rows)
        us = u[rs]
        prev = cv_scr[s]
        u_m1 = jnp.where(rid == 0, prev[7:8, :], pltpu.roll(us, 1, 0))
        u_m2 = jnp.where(rid == 0, prev[6:7, :], jnp.where(rid == 1, prev[7:8, :], pltpu.roll(us, 2, 0)))
        conv_rows.append(cw[0:1, :] * u_m2 + cw[1:2, :] * u_m1 + cw[2:3, :] * us)
        cv_scr[s] = us[rows - 8:rows, :]
    conv_d, cod_ref[0] = _decode_conv(u[prow:], c0_ref[0], cw, DEC_SEQS, dec_len)
    conv_rows.append(conv_d)

    y_a_in = jnp.concatenate(y_a_rows, axis=0)
    z = bg * jnp.concatenate(conv_rows, axis=0)
    out = _merge_out(x, y_a_in, z, gates, wa_ref, wb_ref, wo_ref)
    xo_ref[...] = out[:prow].reshape(nseq, rows, D_MODEL)
    xod_ref[...] = out[prow:]

    @pl.when(t == pl.num_programs(1) - 1)
    def _():
        for s in range(nseq):
            for hd in range(N_HEADS):
                so_ref[0, s, hd] = st_scr[s, hd].T
            co_ref[0, s] = u[(s + 1) * rows - 2:(s + 1) * rows, :]


def _mix(x, xd, s0, c0, lbl, gm, win, cw, gh, wa, wb, wo):
    bsz, seq, _ = x.shape
    dbsz, dec_len, _ = xd.shape
    nseq, rows = MIX_SEQS, MIX_ROWS
    tiles = seq // rows
    assert seq % rows == 0 and bsz % nseq == 0
    assert dbsz == DEC_SEQS * (bsz // nseq) * tiles, "decode sequences are spread evenly over the grid steps"
    assert dec_len % 8 == 0 and dec_len >= CONV_W - 1
    n_in = win.shape[1]
    drows = DEC_SEQS * dec_len
    x_spec = pl.BlockSpec((nseq, rows, D_MODEL), lambda b, t: (b, t, 0))
    xd_spec = pl.BlockSpec((drows, D_MODEL), lambda b, t: (b * tiles + t, 0))
    sd_spec = pl.BlockSpec((1, DEC_SEQS, N_HEADS, HEAD_DIM, HEAD_DIM), lambda b, t: (0, b * tiles + t, 0, 0, 0))
    cd_spec = pl.BlockSpec((1, DEC_SEQS, CONV_W - 1, D_B), lambda b, t: (0, b * tiles + t, 0, 0))
    xo, so, co, xod, sod, cod = pl.pallas_call(
        functools.partial(_mix_kernel, dec_len=dec_len),
        grid=(bsz // nseq, tiles),
        in_specs=[x_spec, xd_spec, sd_spec, cd_spec, _const_spec(lbl.shape), _const_spec((1, D_MODEL)),
                  _const_spec((D_MODEL, n_in)), _const_spec((CONV_W, D_B)), _const_spec((1, D_A)),
                  _const_spec((D_A, D_MODEL)), _const_spec((D_B, D_MODEL)), _const_spec((D_MODEL, D_MODEL))],
        out_specs=[x_spec,
                   pl.BlockSpec((1, nseq, N_HEADS, HEAD_DIM, HEAD_DIM), lambda b, t: (0, b, 0, 0, 0)),
                   pl.BlockSpec((1, nseq, CONV_W - 1, D_B), lambda b, t: (0, b, 0, 0)),
                   xd_spec, sd_spec, cd_spec],
        out_shape=[jax.ShapeDtypeStruct(x.shape, F32),
                   jax.ShapeDtypeStruct((1, bsz, N_HEADS, HEAD_DIM, HEAD_DIM), F32),
                   jax.ShapeDtypeStruct((1, bsz, CONV_W - 1, D_B), F32),
                   jax.ShapeDtypeStruct((dbsz * dec_len, D_MODEL), F32),
                   jax.ShapeDtypeStruct(s0.shape, F32), jax.ShapeDtypeStruct(c0.shape, F32)],
        scratch_shapes=[pltpu.VMEM((nseq, N_HEADS, HEAD_DIM, HEAD_DIM), F32), pltpu.VMEM((nseq, 8, D_B), F32)],
        compiler_params=pltpu.CompilerParams(dimension_semantics=("arbitrary", "arbitrary"),
                                             vmem_limit_bytes=VMEM_LIMIT_BYTES),
        name="mix",
    )(x, xd.reshape(dbsz * dec_len, D_MODEL), s0, c0, lbl, gm, win, cw, gh, wa, wb, wo)
    return xo, so, co, xod.reshape(xd.shape), sod, cod


def kernel(x_prompt, x_sample, state_hgrn, state_conv, lower_bound_logits, g_ffn1, w1_ffn1, w3_ffn1, w2_ffn1,
           g_mix, w_in, conv_w, g_hgrn_out, w_a_out, w_b_out, w_o, g_ffn2, w1_ffn2, w3_ffn2, w2_ffn2, g_final):
    assert state_hgrn.shape[0] == 1, "single-layer trunk"
    sq = lambda w: w.reshape(w.shape[1:])
    ffn1 = (g_ffn1, sq(w1_ffn1), sq(w3_ffn1), sq(w2_ffn1))
    ffn2 = (g_ffn2, sq(w1_ffn2), sq(w3_ffn2), sq(w2_ffn2))
    gfin = g_final.reshape(1, D_MODEL)
    mixw = (lower_bound_logits, g_mix, sq(w_in), sq(conv_w), g_hgrn_out, sq(w_a_out), sq(w_b_out), sq(w_o))

    def ffn(xa, xb, weights, final_norm):
        ya, yb = _ffn(xa.reshape(-1, D_MODEL), xb.reshape(-1, D_MODEL), *weights, gfin, final_norm=final_norm)
        return ya.reshape(xa.shape), yb.reshape(xb.shape)

    xp, xs = ffn(x_prompt, x_sample, ffn1, False)
    xp, sh_p, sc_p, xs, sh_s, sc_s = _mix(xp, xs, state_hgrn, state_conv, *mixw)
    yp, ys = ffn(xp, xs, ffn2, True)
    return yp, ys, sh_p, sc_p, sh_s, sc_s
```

```python
import functools

import jax
import jax.numpy as jnp
from jax import lax
from jax.experimental import pallas as pl
from jax.experimental.pallas import tpu as pltpu

F32 = jnp.float32
BF16 = jnp.bfloat16

D_MODEL = 1024
D_A = 512
HEAD_DIM = 128
N_HEADS = 4
D_B = 512
CONV_W = 3
D_FF = 2816
CHUNK = 32
EPS = 1e-6

VMEM_LIMIT_BYTES = 56 * 1024 * 1024

FFN_ROWS = 512
FFN_COLS = 256
FFN_STAGE_SLOTS = 2
MIX_ROWS = 256
MIX_SEQS = 2
DEC_SEQS = 4
GATE_COLS = 256


def _dot(a, b):
    return jnp.dot(a, b, preferred_element_type=F32)


def _dot_nt(a, b):
    return lax.dot_general(a, b, (((1,), (1,)), ((), ())), preferred_element_type=F32)


def _rmsnorm(x, g):
    return x * lax.rsqrt(jnp.mean(x * x, axis=-1, keepdims=True) + EPS) * g


def _norm_rows(x, g):
    r = lax.rsqrt(jnp.mean(x * x, axis=-1, keepdims=True) + EPS)
    xg = x * g
    return (xg * r).astype(BF16), xg.astype(BF16), r


def _const_spec(shape):
    nd = len(shape)
    return pl.BlockSpec(shape, lambda *_: (0,) * nd, pipeline_mode=pl.Buffered(1))


def _ffn_rows(x_ref, o_ref, g_ref, w1_ref, w3_ref, w2_ref, gf_ref, final_norm, before_chunk=None):
    x = x_ref[...]
    h, h0, r = _norm_rows(x, g_ref[...])
    acc = None
    for c in range(D_FF // FFN_COLS):
        sl = slice(c * FFN_COLS, (c + 1) * FFN_COLS)
        if before_chunk is not None:
            before_chunk(c)
        if c == 0:
            a = r * _dot(h0, w1_ref[:, sl])
            b = r * _dot(h0, w3_ref[:, sl])
        else:
            a = _dot(h, w1_ref[:, sl])
            b = _dot(h, w3_ref[:, sl])
        act = (a * jax.nn.sigmoid(a) * b).astype(BF16)
        part = _dot(act, w2_ref[sl, :])
        if before_chunk is None:
            acc = part if acc is None else acc + part
        elif c == 0:
            o_ref[...] = part
        else:
            o_ref[...] += part
    y = x_ref[...] + 0.5 * (o_ref[...] if acc is None else acc)
    if final_norm:
        y = _rmsnorm(y, gf_ref[...])
    o_ref[...] = y


def _ffn_chunk_copies(hbm, stage, sem, c):
    (w1_hbm, w3_hbm, w2_hbm), (s1_ref, s3_ref, s2_ref) = hbm, stage
    sl = pl.ds(c * FFN_COLS, FFN_COLS)
    slot = c % FFN_STAGE_SLOTS
    return (pltpu.make_async_copy(w1_hbm.at[:, sl], s1_ref.at[slot], sem.at[slot, 0]),
            pltpu.make_async_copy(w3_hbm.at[:, sl], s3_ref.at[slot], sem.at[slot, 1]),
            pltpu.make_async_copy(w2_hbm.at[sl, :], s2_ref.at[slot], sem.at[slot, 2]))


def _ffn_kernel(xa_ref, xb_ref, g_ref, w1_hbm, w3_hbm, w2_hbm, gf_ref, oa_ref, ob_ref,
                w1_ref, w3_ref, w2_ref, s1_ref, s3_ref, s2_ref, sem, *, steps_a, final_norm):
    i = pl.program_id(0)
    n_chunks = D_FF // FFN_COLS
    weights = (w1_ref, w3_ref, w2_ref)
    stage = (s1_ref, s3_ref, s2_ref)
    copies = functools.partial(_ffn_chunk_copies, (w1_hbm, w3_hbm, w2_hbm), stage, sem)

    @pl.when(i == 0)
    def _():
        for c in range(min(FFN_STAGE_SLOTS, n_chunks)):
            for cp in copies(c):
                cp.start()

        def arrive(c):
            sl = slice(c * FFN_COLS, (c + 1) * FFN_COLS)
            slot = c % FFN_STAGE_SLOTS
            for cp in copies(c):
                cp.wait()
            w1_ref[:, sl] = s1_ref[slot].astype(BF16)
            w3_ref[:, sl] = s3_ref[slot].astype(BF16)
            w2_ref[sl, :] = s2_ref[slot].astype(BF16)
            if c + FFN_STAGE_SLOTS < n_chunks:
                for cp in copies(c + FFN_STAGE_SLOTS):
                    cp.start()

        _ffn_rows(xa_ref, oa_ref, g_ref, *weights, gf_ref, final_norm, before_chunk=arrive)

    @pl.when((i > 0) & (i < steps_a))
    def _():
        _ffn_rows(xa_ref, oa_ref, g_ref, *weights, gf_ref, final_norm)

    @pl.when(i >= steps_a)
    def _():
        _ffn_rows(xb_ref, ob_ref, g_ref, *weights, gf_ref, final_norm)


def _ffn(xa, xb, g, w1, w3, w2, g_final, *, final_norm):
    rows = FFN_ROWS
    na, nb = xa.shape[0], xb.shape[0]
    assert na % rows == 0 and nb % rows == 0
    steps_a, steps_b = na // rows, nb // rows
    assert steps_a >= 1
    a_spec = pl.BlockSpec((rows, D_MODEL), lambda i: (jnp.minimum(i, steps_a - 1), 0))
    b_spec = pl.BlockSpec((rows, D_MODEL), lambda i: (jnp.maximum(i - steps_a, 0), 0))
    hbm_spec = pl.BlockSpec(memory_space=pl.ANY)
    slots = FFN_STAGE_SLOTS
    return pl.pallas_call(
        functools.partial(_ffn_kernel, steps_a=steps_a, final_norm=final_norm),
        grid=(steps_a + steps_b,),
        in_specs=[a_spec, b_spec, _const_spec((1, D_MODEL)), hbm_spec, hbm_spec, hbm_spec,
                  _const_spec((1, D_MODEL))],
        out_specs=[a_spec, b_spec],
        out_shape=[jax.ShapeDtypeStruct((na, D_MODEL), F32), jax.ShapeDtypeStruct((nb, D_MODEL), F32)],
        scratch_shapes=[pltpu.VMEM((D_MODEL, D_FF), BF16), pltpu.VMEM((D_MODEL, D_FF), BF16),
                        pltpu.VMEM((D_FF, D_MODEL), BF16),
                        pltpu.VMEM((slots, D_MODEL, FFN_COLS), F32), pltpu.VMEM((slots, D_MODEL, FFN_COLS), F32),
                        pltpu.VMEM((slots, FFN_COLS, D_MODEL), F32), pltpu.SemaphoreType.DMA((slots, 3))],
        compiler_params=pltpu.CompilerParams(dimension_semantics=("arbitrary",),
                                             vmem_limit_bytes=VMEM_LIMIT_BYTES),
        name="ffn_final" if final_norm else "ffn",
    )(xa, xb, g, w1, w3, w2, g_final)


def _lower_bound(lbl):
    m = jnp.max(lbl, axis=0, keepdims=True)
    e = jnp.exp(lbl - m)
    return e[0:1, :] / jnp.sum(e, axis=0, keepdims=True)


def _causal_in_chunk(rows, chunk, key_major=False):
    shift = chunk.bit_length() - 1
    assert 1 << shift == chunk
    r = lax.broadcasted_iota(jnp.int32, (rows, rows), 1 if key_major else 0)
    c = lax.broadcasted_iota(jnp.int32, (rows, rows), 0 if key_major else 1)
    return (lax.shift_right_logical(r, shift) == lax.shift_right_logical(c, shift)) & (c <= r)


def _chunk_cumsum(x, chunk):
    pos = lax.broadcasted_iota(jnp.int32, x.shape, 0) & (chunk - 1)
    step = 1
    while step < chunk:
        x = x + jnp.where(pos >= step, pltpu.roll(x, step, 0), 0.0)
        step *= 2
    return x


def _chunk_last(x, chunk):
    rows, width = x.shape
    return jnp.concatenate(
        [jnp.broadcast_to(x[n * chunk + chunk - 1:(n + 1) * chunk, :], (chunk, width))
         for n in range(rows // chunk)], axis=0)


def _gate_products(h, h0, r, win_ref):
    pf = r * _dot(h0, win_ref[:, 1 * D_A:2 * D_A].astype(BF16))
    pq = _dot(h, win_ref[:, 0 * D_A:1 * D_A].astype(BF16))
    pv = _dot(h, win_ref[:, 2 * D_A:3 * D_A].astype(BF16))
    og = _dot(h, win_ref[:, 3 * D_A:4 * D_A].astype(BF16))
    return pq, pf, pv, og


def _decay_chain(pq, pf, lb, chunk):
    one_m = 1.0 - lb
    sig = jax.nn.sigmoid(pf)
    f = lb + one_m * sig
    k_in = one_m * (1.0 - sig)
    b = _chunk_cumsum(jnp.log(f), chunk)
    blast = _chunk_last(b, chunk)
    q_d = pq * jnp.exp(b)
    k_d = k_in * jnp.exp(-b)
    k_w = k_in * jnp.exp(blast - b)
    return q_d, k_d, k_w, blast


def _intra(q_b, k_d_h, v_h, tril):
    scores = jnp.where(tril, _dot_nt(q_b, k_d_h.astype(BF16)), 0.0)
    return _dot(scores.astype(BF16), v_h.astype(BF16))


def _head_out(o, gh_h, og_h):
    o = o * lax.rsqrt(jnp.mean(o * o, axis=-1, keepdims=True) + EPS)
    return o * gh_h * (og_h * jax.nn.sigmoid(og_h))


def _chunk_blocks(x_h, rows, chunk):
    shift = chunk.bit_length() - 1
    rid = lax.shift_right_logical(lax.broadcasted_iota(jnp.int32, (rows, HEAD_DIM), 0), shift)
    xb = x_h.astype(BF16)
    zero = jnp.zeros_like(xb)
    return jnp.concatenate([jnp.where(rid == n, xb, zero) for n in range(rows // chunk)], axis=1)


def _conv_inputs(h, win_ref):
    bg = _dot(h, win_ref[:, 4 * D_A:4 * D_A + D_B].astype(BF16))
    cg = _dot(h, win_ref[:, 4 * D_A + D_B:4 * D_A + 2 * D_B].astype(BF16))
    vv = _dot(h, win_ref[:, 4 * D_A + 2 * D_B:4 * D_A + 3 * D_B].astype(BF16))
    return bg, cg * vv


def _gate_piece(h, win_ref, j):
    lo = 4 * D_A + 3 * D_B + j * GATE_COLS
    return jax.nn.sigmoid(_dot(h, win_ref[:, lo:lo + GATE_COLS].astype(BF16)))


def _merge_out(x, y_a_in, z, gates, wa_ref, wb_ref, wo_ref):
    half = D_MODEL // GATE_COLS
    y_a = _dot(y_a_in.astype(BF16), wa_ref[...].astype(BF16))
    y_b = _dot(z.astype(BF16), wb_ref[...].astype(BF16))
    merged = jnp.concatenate(gates[:half], axis=1) * y_a + jnp.concatenate(gates[half:], axis=1) * y_b
    return x + _dot(merged.astype(BF16), wo_ref[...].astype(BF16))


def _decode_recurrence(q_d, k_d, k_w, v, blast, s0_ref, so_ref, nseq, chunk):
    rows = nseq * chunk
    tril = _causal_in_chunk(rows, chunk)
    outs = []
    for hd in range(N_HEADS):
        sl = slice(hd * HEAD_DIM, (hd + 1) * HEAD_DIM)
        q_b = q_d[:, sl].astype(BF16)
        o_intra = _intra(q_b, k_d[:, sl], v[:, sl], tril)
        ds = _dot(k_w[:, sl].T.astype(BF16), _chunk_blocks(v[:, sl], rows, chunk))
        decay_t = jnp.exp(blast[:, sl]).T
        inter = []
        for j in range(nseq):
            s0 = s0_ref[0, j, hd]
            inter.append(_dot(q_b[j * chunk:(j + 1) * chunk], s0.astype(BF16)))
            so_ref[0, j, hd] = (s0 * decay_t[:, j * chunk:j * chunk + 1]
                                + ds[:, j * HEAD_DIM:(j + 1) * HEAD_DIM])
        outs.append(o_intra + jnp.concatenate(inter, axis=0))
    return outs


def _decode_conv(u, c0, cw, nseq, chunk):
    tid = lax.broadcasted_iota(jnp.int32, (nseq, chunk, D_B), 1)
    u3 = u.reshape(nseq, chunk, D_B)
    r1 = pltpu.roll(u, 1, 0).reshape(nseq, chunk, D_B)
    r2 = pltpu.roll(u, 2, 0).reshape(nseq, chunk, D_B)
    u_m1 = jnp.where(tid == 0, c0[:, 1:2, :], r1)
    u_m2 = jnp.where(tid == 0, c0[:, 0:1, :], jnp.where(tid == 1, c0[:, 1:2, :], r2))
    conv = cw[0:1, :] * u_m2 + cw[1:2, :] * u_m1 + cw[2:3, :] * u3
    return conv.reshape(nseq * chunk, D_B), u3[:, chunk - 2:chunk, :]


def _mix_kernel(x_ref, xd_ref, s0_ref, c0_ref, lbl_ref, gm_ref, win_ref, cw_ref, gh_ref, wa_ref, wb_ref, wo_ref,
                xo_ref, so_ref, co_ref, xod_ref, sod_ref, cod_ref, st_scr, cv_scr, *, dec_len):
    nseq, rows, chunk = MIX_SEQS, MIX_ROWS, CHUNK
    prow = nseq * rows
    t = pl.program_id(1)

    @pl.when(t == 0)
    def _():
        st_scr[...] = jnp.zeros_like(st_scr)
        cv_scr[...] = jnp.zeros_like(cv_scr)

    x = jnp.concatenate([x_ref[...].reshape(prow, D_MODEL), xd_ref[...]], axis=0)
    h, h0, r = _norm_rows(x, gm_ref[...])
    lb = _lower_bound(lbl_ref[...])
    pq, pf, v, og = _gate_products(h, h0, r, win_ref)
    q_d, k_d, k_w, blast = _decay_chain(pq[:prow], pf[:prow], lb, chunk)
    qd_d, kd_d, kw_d, blast_d = _decay_chain(pq[prow:], pf[prow:], lb, dec_len)
    triu = _causal_in_chunk(rows, chunk, key_major=True)
    gh = gh_ref[...]
    bg, u = _conv_inputs(h, win_ref)
    cw = cw_ref[...]
    rid = lax.broadcasted_iota(jnp.int32, (rows, D_B), 0)

    units = [(s, hd, slice(s * rows, (s + 1) * rows), slice(hd * HEAD_DIM, (hd + 1) * HEAD_DIM))
             for s in range(nseq) for hd in range(N_HEADS)]
    q_b = [q_d[rs, sl].astype(BF16) for _, _, rs, sl in units]
    v_t = [v[rs, sl].T.astype(BF16) for _, _, rs, sl in units]
    scores_t = [_dot_nt(k_d[rs, sl].astype(BF16), q_b[i]) for i, (_, _, rs, sl) in enumerate(units)]
    ds_t = [_dot(v_t[i], _chunk_blocks(k_w[rs, sl], rows, chunk)) for i, (_, _, rs, sl) in enumerate(units)]
    o_intra = [_dot(v_t[i], jnp.where(triu, scores_t[i], 0.0).astype(BF16)).T for i in range(len(units))]
    s_t = [st_scr[s, hd] for s, hd, _, _ in units]
    inter = [[] for _ in units]
    assert 2 * D_MODEL // GATE_COLS == rows // chunk
    gates = []
    for n in range(rows // chunk):
        gates.append(_gate_piece(h, win_ref, n))
        for i, (_, _, rs, sl) in enumerate(units):
            inter[i].append(_dot(q_b[i][n * chunk:(n + 1) * chunk], s_t[i].T.astype(BF16)))
            decay = jnp.exp(blast[rs, sl][n * chunk:n * chunk + 1, :])
            s_t[i] = s_t[i] * decay + ds_t[i][:, n * HEAD_DIM:(n + 1) * HEAD_DIM]
    o_dec = _decode_recurrence(qd_d, kd_d, kw_d, v[prow:], blast_d, s0_ref, sod_ref, DEC_SEQS, dec_len)
    heads = []
    for i, (s, hd, rs, sl) in enumerate(units):
        st_scr[s, hd] = s_t[i]
        o = o_intra[i] + jnp.concatenate(inter[i], axis=0)
        heads.append(_head_out(o, gh[:, sl], og[rs, sl]))
    y_a_rows = [jnp.concatenate(heads[s * N_HEADS:(s + 1) * N_HEADS], axis=1) for s in range(nseq)]
    y_a_rows.append(jnp.concatenate(
        [_head_out(o_dec[hd], gh[:, hd * HEAD_DIM:(hd + 1) * HEAD_DIM], og[prow:, hd * HEAD_DIM:(hd + 1) * HEAD_DIM])
         for hd in range(N_HEADS)], axis=1))

    conv_rows = []
    for s in range(nseq):
        rs = slice(s * rows, (s + 1) * rows)
        us = u[rs]
        prev = cv_scr[s]
        u_m1 = jnp.where(rid == 0, prev[7:8, :], pltpu.roll(us, 1, 0))
        u_m2 = jnp.where(rid == 0, prev[6:7, :], jnp.where(rid == 1, prev[7:8, :], pltpu.roll(us, 2, 0)))
        conv_rows.append(cw[0:1, :] * u_m2 + cw[1:2, :] * u_m1 + cw[2:3, :] * us)
        cv_scr[s] = us[rows - 8:rows, :]
    conv_d, cod_ref[0] = _decode_conv(u[prow:], c0_ref[0], cw, DEC_SEQS, dec_len)
    conv_rows.append(conv_d)

    y_a_in = jnp.concatenate(y_a_rows, axis=0)
    z = bg * jnp.concatenate(conv_rows, axis=0)
    out = _merge_out(x, y_a_in, z, gates, wa_ref, wb_ref, wo_ref)
    xo_ref[...] = out[:prow].reshape(nseq, rows, D_MODEL)
    xod_ref[...] = out[prow:]

    @pl.when(t == pl.num_programs(1) - 1)
    def _():
        for s in range(nseq):
            for hd in range(N_HEADS):
                so_ref[0, s, hd] = st_scr[s, hd].T
            co_ref[0, s] = u[(s + 1) * rows - 2:(s + 1) * rows, :]


def _mix(x, xd, s0, c0, lbl, gm, win, cw, gh, wa, wb, wo):
    bsz, seq, _ = x.shape
    dbsz, dec_len, _ = xd.shape
    nseq, rows = MIX_SEQS, MIX_ROWS
    tiles = seq // rows
    assert seq % rows == 0 and bsz % nseq == 0
    assert dbsz == DEC_SEQS * (bsz // nseq) * tiles, "decode sequences are spread evenly over the grid steps"
    assert dec_len % 8 == 0 and dec_len >= CONV_W - 1
    n_in = win.shape[1]
    drows = DEC_SEQS * dec_len
    x_spec = pl.BlockSpec((nseq, rows, D_MODEL), lambda b, t: (b, t, 0))
    xd_spec = pl.BlockSpec((drows, D_MODEL), lambda b, t: (b * tiles + t, 0))
    sd_spec = pl.BlockSpec((1, DEC_SEQS, N_HEADS, HEAD_DIM, HEAD_DIM), lambda b, t: (0, b * tiles + t, 0, 0, 0))
    cd_spec = pl.BlockSpec((1, DEC_SEQS, CONV_W - 1, D_B), lambda b, t: (0, b * tiles + t, 0, 0))
    xo, so, co, xod, sod, cod = pl.pallas_call(
        functools.partial(_mix_kernel, dec_len=dec_len),
        grid=(bsz // nseq, tiles),
        in_specs=[x_spec, xd_spec, sd_spec, cd_spec, _const_spec(lbl.shape), _const_spec((1, D_MODEL)),
                  _const_spec((D_MODEL, n_in)), _const_spec((CONV_W, D_B)), _const_spec((1, D_A)),
                  _const_spec((D_A, D_MODEL)), _const_spec((D_B, D_MODEL)), _const_spec((D_MODEL, D_MODEL))],
        out_specs=[x_spec,
                   pl.BlockSpec((1, nseq, N_HEADS, HEAD_DIM, HEAD_DIM), lambda b, t: (0, b, 0, 0, 0)),
                   pl.BlockSpec((1, nseq, CONV_W - 1, D_B), lambda b, t: (0, b, 0, 0)),
                   xd_spec, sd_spec, cd_spec],
        out_shape=[jax.ShapeDtypeStruct(x.shape, F32),
                   jax.ShapeDtypeStruct((1, bsz, N_HEADS, HEAD_DIM, HEAD_DIM), F32),
                   jax.ShapeDtypeStruct((1, bsz, CONV_W - 1, D_B), F32),
                   jax.ShapeDtypeStruct((dbsz * dec_len, D_MODEL), F32),
                   jax.ShapeDtypeStruct(s0.shape, F32), jax.ShapeDtypeStruct(c0.shape, F32)],
        scratch_shapes=[pltpu.VMEM((nseq, N_HEADS, HEAD_DIM, HEAD_DIM), F32), pltpu.VMEM((nseq, 8, D_B), F32)],
        compiler_params=pltpu.CompilerParams(dimension_semantics=("arbitrary", "arbitrary"),
                                             vmem_limit_bytes=VMEM_LIMIT_BYTES),
        name="mix",
    )(x, xd.reshape(dbsz * dec_len, D_MODEL), s0, c0, lbl, gm, win, cw, gh, wa, wb, wo)
    return xo, so, co, xod.reshape(xd.shape), sod, cod


def kernel(x_prompt, x_sample, state_hgrn, state_conv, lower_bound_logits, g_ffn1, w1_ffn1, w3_ffn1, w2_ffn1,
           g_mix, w_in, conv_w, g_hgrn_out, w_a_out, w_b_out, w_o, g_ffn2, w1_ffn2, w3_ffn2, w2_ffn2, g_final):
    assert state_hgrn.shape[0] == 1, "single-layer trunk"
    sq = lambda w: w.reshape(w.shape[1:])
    ffn1 = (g_ffn1, sq(w1_ffn1), sq(w3_ffn1), sq(w2_ffn1))
    ffn2 = (g_ffn2, sq(w1_ffn2), sq(w3_ffn2), sq(w2_ffn2))
    gfin = g_final.reshape(1, D_MODEL)
    mixw = (lower_bound_logits, g_mix, sq(w_in), sq(conv_w), g_hgrn_out, sq(w_a_out), sq(w_b_out), sq(w_o))

    def ffn(xa, xb, weights, final_norm):
        ya, yb = _ffn(xa.reshape(-1, D_MODEL), xb.reshape(-1, D_MODEL), *weights, gfin, final_norm=final_norm)
        return ya.reshape(xa.shape), yb.reshape(xb.shape)

    xp, xs = ffn(x_prompt, x_sample, ffn1, False)
    xp, sh_p, sc_p, xs, sh_s, sc_s = _mix(xp, xs, state_hgrn, state_conv, *mixw)
    yp, ys = ffn(xp, xs, ffn2, True)
    return yp, ys, sh_p, sc_p, sh_s, sc_s
```

```python
import functools

import jax
import jax.numpy as jnp
from jax import lax
from jax.experimental import pallas as pl
from jax.experimental.pallas import tpu as pltpu

F32 = jnp.float32
BF16 = jnp.bfloat16

D_MODEL = 1024
D_A = 512
HEAD_DIM = 128
N_HEADS = 4
D_B = 512
CONV_W = 3
D_FF = 2816
CHUNK = 32
EPS = 1e-6

VMEM_LIMIT_BYTES = 56 * 1024 * 1024

FFN_ROWS = 512
FFN_COLS = 256
FFN_STAGE_SLOTS = 2
MIX_ROWS = 256
MIX_SEQS = 2
DEC_SEQS = 4
GATE_COLS = 256


def _dot(a, b):
    return jnp.dot(a, b, preferred_element_type=F32)


def _dot_nt(a, b):
    return lax.dot_general(a, b, (((1,), (1,)), ((), ())), preferred_element_type=F32)


def _rmsnorm(x, g):
    return x * lax.rsqrt(jnp.mean(x * x, axis=-1, keepdims=True) + EPS) * g


def _norm_rows(x, g):
    r = lax.rsqrt(jnp.mean(x * x, axis=-1, keepdims=True) + EPS)
    xg = x * g
    return (xg * r).astype(BF16), xg.astype(BF16), r


def _const_spec(shape):
    nd = len(shape)
    return pl.BlockSpec(shape, lambda *_: (0,) * nd, pipeline_mode=pl.Buffered(1))


def _ffn_rows(x_ref, o_ref, g_ref, w1_ref, w3_ref, w2_ref, gf_ref, final_norm, before_chunk=None):
    x = x_ref[...]
    h, h0, r = _norm_rows(x, g_ref[...])
    acc = None
    for c in range(D_FF // FFN_COLS):
        sl = slice(c * FFN_COLS, (c + 1) * FFN_COLS)
        if before_chunk is not None:
            before_chunk(c)
        if c == 0:
            a = r * _dot(h0, w1_ref[:, sl])
            b = r * _dot(h0, w3_ref[:, sl])
        else:
            a = _dot(h, w1_ref[:, sl])
            b = _dot(h, w3_ref[:, sl])
        act = (a * jax.nn.sigmoid(a) * b).astype(BF16)
        part = _dot(act, w2_ref[sl, :])
        if before_chunk is None:
            acc = part if acc is None else acc + part
        elif c == 0:
            o_ref[...] = part
        else:
            o_ref[...] += part
    y = x_ref[...] + 0.5 * (o_ref[...] if acc is None else acc)
    if final_norm:
        y = _rmsnorm(y, gf_ref[...])
    o_ref[...] = y


def _ffn_chunk_copies(hbm, stage, sem, c):
    (w1_hbm, w3_hbm, w2_hbm), (s1_ref, s3_ref, s2_ref) = hbm, stage
    sl = pl.ds(c * FFN_COLS, FFN_COLS)
    slot = c % FFN_STAGE_SLOTS
    return (pltpu.make_async_copy(w1_hbm.at[:, sl], s1_ref.at[slot], sem.at[slot, 0]),
            pltpu.make_async_copy(w3_hbm.at[:, sl], s3_ref.at[slot], sem.at[slot, 1]),
            pltpu.make_async_copy(w2_hbm.at[sl, :], s2_ref.at[slot], sem.at[slot, 2]))


def _ffn_kernel(xa_ref, xb_ref, g_ref, w1_hbm, w3_hbm, w2_hbm, gf_ref, oa_ref, ob_ref,
                w1_ref, w3_ref, w2_ref, s1_ref, s3_ref, s2_ref, sem, *, steps_a, final_norm):
    i = pl.program_id(0)
    n_chunks = D_FF // FFN_COLS
    weights = (w1_ref, w3_ref, w2_ref)
    stage = (s1_ref, s3_ref, s2_ref)
    copies = functools.partial(_ffn_chunk_copies, (w1_hbm, w3_hbm, w2_hbm), stage, sem)

    @pl.when(i == 0)
    def _():
        for c in range(min(FFN_STAGE_SLOTS, n_chunks)):
            for cp in copies(c):
                cp.start()

        def arrive(c):
            sl = slice(c * FFN_COLS, (c + 1) * FFN_COLS)
            slot = c % FFN_STAGE_SLOTS
            for cp in copies(c):
                cp.wait()
            w1_ref[:, sl] = s1_ref[slot].astype(BF16)
            w3_ref[:, sl] = s3_ref[slot].astype(BF16)
            w2_ref[sl, :] = s2_ref[slot].astype(BF16)
            if c + FFN_STAGE_SLOTS < n_chunks:
                for cp in copies(c + FFN_STAGE_SLOTS):
                    cp.start()

        _ffn_rows(xa_ref, oa_ref, g_ref, *weights, gf_ref, final_norm, before_chunk=arrive)

    @pl.when((i > 0) & (i < steps_a))
    def _():
        _ffn_rows(xa_ref, oa_ref, g_ref, *weights, gf_ref, final_norm)

    @pl.when(i >= steps_a)
    def _():
        _ffn_rows(xb_ref, ob_ref, g_ref, *weights, gf_ref, final_norm)


def _ffn(xa, xb, g, w1, w3, w2, g_final, *, final_norm):
    rows = FFN_ROWS
    na, nb = xa.shape[0], xb.shape[0]
    assert na % rows == 0 and nb % rows == 0
    steps_a, steps_b = na // rows, nb // rows
    assert steps_a >= 1
    a_spec = pl.BlockSpec((rows, D_MODEL), lambda i: (jnp.minimum(i, steps_a - 1), 0))
    b_spec = pl.BlockSpec((rows, D_MODEL), lambda i: (jnp.maximum(i - steps_a, 0), 0))
    hbm_spec = pl.BlockSpec(memory_space=pl.ANY)
    slots = FFN_STAGE_SLOTS
    return pl.pallas_call(
        functools.partial(_ffn_kernel, steps_a=steps_a, final_norm=final_norm),
        grid=(steps_a + steps_b,),
        in_specs=[a_spec, b_spec, _const_spec((1, D_MODEL)), hbm_spec, hbm_spec, hbm_spec,
                  _const_spec((1, D_MODEL))],
        out_specs=[a_spec, b_spec],
        out_shape=[jax.ShapeDtypeStruct((na, D_MODEL), F32), jax.ShapeDtypeStruct((nb, D_MODEL), F32)],
        scratch_shapes=[pltpu.VMEM((D_MODEL, D_FF), BF16), pltpu.VMEM((D_MODEL, D_FF), BF16),
                        pltpu.VMEM((D_FF, D_MODEL), BF16),
                        pltpu.VMEM((slots, D_MODEL, FFN_COLS), F32), pltpu.VMEM((slots, D_MODEL, FFN_COLS), F32),
                        pltpu.VMEM((slots, FFN_COLS, D_MODEL), F32), pltpu.SemaphoreType.DMA((slots, 3))],
        compiler_params=pltpu.CompilerParams(dimension_semantics=("arbitrary",),
                                             vmem_limit_bytes=VMEM_LIMIT_BYTES),
        name="ffn_final" if final_norm else "ffn",
    )(xa, xb, g, w1, w3, w2, g_final)


def _lower_bound(lbl):
    m = jnp.max(lbl, axis=0, keepdims=True)
    e = jnp.exp(lbl - m)
    return e[0:1, :] / jnp.sum(e, axis=0, keepdims=True)


def _causal_in_chunk(rows, chunk, key_major=False):
    shift = chunk.bit_length() - 1
    assert 1 << shift == chunk
    r = lax.broadcasted_iota(jnp.int32, (rows, rows), 1 if key_major else 0)
    c = lax.broadcasted_iota(jnp.int32, (rows, rows), 0 if key_major else 1)
    return (lax.shift_right_logical(r, shift) == lax.shift_right_logical(c, shift)) & (c <= r)


def _chunk_cumsum(x, chunk):
    pos = lax.broadcasted_iota(jnp.int32, x.shape, 0) & (chunk - 1)
    step = 1
    while step < chunk:
        x = x + jnp.where(pos >= step, pltpu.roll(x, step, 0), 0.0)
        step *= 2
    return x


def _chunk_last(x, chunk):
    rows, width = x.shape
    return jnp.concatenate(
        [jnp.broadcast_to(x[n * chunk + chunk - 1:(n + 1) * chunk, :], (chunk, width))
         for n in range(rows // chunk)], axis=0)


def _gate_products(h, h0, r, win_ref):
    pf = r * _dot(h0, win_ref[:, 1 * D_A:2 * D_A].astype(BF16))
    pq = _dot(h, win_ref[:, 0 * D_A:1 * D_A].astype(BF16))
    pv = _dot(h, win_ref[:, 2 * D_A:3 * D_A].astype(BF16))
    og = _dot(h, win_ref[:, 3 * D_A:4 * D_A].astype(BF16))
    return pq, pf, pv, og


def _decay_chain(pq, pf, lb, chunk):
    one_m = 1.0 - lb
    sig = jax.nn.sigmoid(pf)
    f = lb + one_m * sig
    k_in = one_m * (1.0 - sig)
    b = _chunk_cumsum(jnp.log(f), chunk)
    blast = _chunk_last(b, chunk)
    q_d = pq * jnp.exp(b)
    k_d = k_in * jnp.exp(-b)
    k_w = k_in * jnp.exp(blast - b)
    return q_d, k_d, k_w, blast


def _intra(q_b, k_d_h, v_h, tril):
    scores = jnp.where(tril, _dot_nt(q_b, k_d_h.astype(BF16)), 0.0)
    return _dot(scores.astype(BF16), v_h.astype(BF16))


def _head_out(o, gh_h, og_h):
    o = o * lax.rsqrt(jnp.mean(o * o, axis=-1, keepdims=True) + EPS)
    return o * gh_h * (og_h * jax.nn.sigmoid(og_h))


def _chunk_blocks(x_h, rows, chunk):
    shift = chunk.bit_length() - 1
    rid = lax.shift_right_logical(lax.broadcasted_iota(jnp.int32, (rows, HEAD_DIM), 0), shift)
    xb = x_h.astype(BF16)
    zero = jnp.zeros_like(xb)
    return jnp.concatenate([jnp.where(rid == n, xb, zero) for n in range(rows // chunk)], axis=1)


def _conv_inputs(h, win_ref):
    bg = _dot(h, win_ref[:, 4 * D_A:4 * D_A + D_B].astype(BF16))
    cg = _dot(h, win_ref[:, 4 * D_A + D_B:4 * D_A + 2 * D_B].astype(BF16))
    vv = _dot(h, win_ref[:, 4 * D_A + 2 * D_B:4 * D_A + 3 * D_B].astype(BF16))
    return bg, cg * vv


def _gate_piece(h, win_ref, j):
    lo = 4 * D_A + 3 * D_B + j * GATE_COLS
    return jax.nn.sigmoid(_dot(h, win_ref[:, lo:lo + GATE_COLS].astype(BF16)))


def _merge_out(x, y_a_in, z, gates, wa_ref, wb_ref, wo_ref):
    half = D_MODEL // GATE_COLS
    y_a = _dot(y_a_in.astype(BF16), wa_ref[...].astype(BF16))
    y_b = _dot(z.astype(BF16), wb_ref[...].astype(BF16))
    merged = jnp.concatenate(gates[:half], axis=1) * y_a + jnp.concatenate(gates[half:], axis=1) * y_b
    return x + _dot(merged.astype(BF16), wo_ref[...].astype(BF16))


def _decode_head(hd, q_d, k_d, k_w, v, blast, tril, s0_ref, so_ref, nseq, chunk):
    rows = nseq * chunk
    sl = slice(hd * HEAD_DIM, (hd + 1) * HEAD_DIM)
    q_b = q_d[:, sl].astype(BF16)
    o_intra = _intra(q_b, k_d[:, sl], v[:, sl], tril)
    ds = _dot(k_w[:, sl].T.astype(BF16), _chunk_blocks(v[:, sl], rows, chunk))
    decay_t = jnp.exp(blast[:, sl]).T
    inter = []
    for j in range(nseq):
        s0 = s0_ref[0, j, hd]
        inter.append(_dot(q_b[j * chunk:(j + 1) * chunk], s0.astype(BF16)))
        so_ref[0, j, hd] = (s0 * decay_t[:, j * chunk:j * chunk + 1]
                            + ds[:, j * HEAD_DIM:(j + 1) * HEAD_DIM])
    return o_intra + jnp.concatenate(inter, axis=0)


def _decode_conv(u, c0, cw, nseq, chunk):
    tid = lax.broadcasted_iota(jnp.int32, (nseq, chunk, D_B), 1)
    u3 = u.reshape(nseq, chunk, D_B)
    r1 = pltpu.roll(u, 1, 0).reshape(nseq, chunk, D_B)
    r2 = pltpu.roll(u, 2, 0).reshape(nseq, chunk, D_B)
    u_m1 = jnp.where(tid == 0, c0[:, 1:2, :], r1)
    u_m2 = jnp.where(tid == 0, c0[:, 0:1, :], jnp.where(tid == 1, c0[:, 1:2, :], r2))
    conv = cw[0:1, :] * u_m2 + cw[1:2, :] * u_m1 + cw[2:3, :] * u3
    return conv.reshape(nseq * chunk, D_B), u3[:, chunk - 2:chunk, :]


def _mix_kernel(x_ref, xd_ref, s0_ref, c0_ref, lbl_ref, gm_ref, win_ref, cw_ref, gh_ref, wa_ref, wb_ref, wo_ref,
                xo_ref, so_ref, co_ref, xod_ref, sod_ref, cod_ref, st_scr, cv_scr, *, dec_len):
    nseq, rows, chunk = MIX_SEQS, MIX_ROWS, CHUNK
    prow = nseq * rows
    t = pl.program_id(1)

    @pl.when(t == 0)
    def _():
        st_scr[...] = jnp.zeros_like(st_scr)
        cv_scr[...] = jnp.zeros_like(cv_scr)

    x = jnp.concatenate([x_ref[...].reshape(prow, D_MODEL), xd_ref[...]], axis=0)
    h, h0, r = _norm_rows(x, gm_ref[...])
    lb = _lower_bound(lbl_ref[...])
    pq, pf, v, og = _gate_products(h, h0, r, win_ref)
    q_d, k_d, k_w, blast = _decay_chain(pq[:prow], pf[:prow], lb, chunk)
    qd_d, kd_d, kw_d, blast_d = _decay_chain(pq[prow:], pf[prow:], lb, dec_len)
    triu = _causal_in_chunk(rows, chunk, key_major=True)
    gh = gh_ref[...]
    bg, u = _conv_inputs(h, win_ref)
    cw = cw_ref[...]
    rid = lax.broadcasted_iota(jnp.int32, (rows, D_B), 0)

    units = [(s, hd, slice(s * rows, (s + 1) * rows), slice(hd * HEAD_DIM, (hd + 1) * HEAD_DIM))
             for s in range(nseq) for hd in range(N_HEADS)]
    q_b = [q_d[rs, sl].astype(BF16) for _, _, rs, sl in units]
    v_t = [v[rs, sl].T.astype(BF16) for _, _, rs, sl in units]
    scores_t = [_dot_nt(k_d[rs, sl].astype(BF16), q_b[i]) for i, (_, _, rs, sl) in enumerate(units)]
    ds_t = [_dot(v_t[i], _chunk_blocks(k_w[rs, sl], rows, chunk)) for i, (_, _, rs, sl) in enumerate(units)]
    o_intra = [_dot(v_t[i], jnp.where(triu, scores_t[i], 0.0).astype(BF16)).T for i in range(len(units))]
    s_t = [st_scr[s, hd] for s, hd, _, _ in units]
    inter = [[] for _ in units]
    assert 2 * D_MODEL // GATE_COLS == rows // chunk
    gates = []
    tril_d = _causal_in_chunk(DEC_SEQS * dec_len, dec_len)
    o_dec = []
    for n in range(rows // chunk):
        gates.append(_gate_piece(h, win_ref, n))
        if n < N_HEADS:
            o_dec.append(_decode_head(n, qd_d, kd_d, kw_d, v[prow:], blast_d, tril_d, s0_ref, sod_ref,
                                      DEC_SEQS, dec_len))
        for i, (_, _, rs, sl) in enumerate(units):
            inter[i].append(_dot(q_b[i][n * chunk:(n + 1) * chunk], s_t[i].T.astype(BF16)))
            decay = jnp.exp(blast[rs, sl][n * chunk:n * chunk + 1, :])
            s_t[i] = s_t[i] * decay + ds_t[i][:, n * HEAD_DIM:(n + 1) * HEAD_DIM]
    assert len(o_dec) == N_HEADS
    heads = []
    for i, (s, hd, rs, sl) in enumerate(units):
        st_scr[s, hd] = s_t[i]
        o = o_intra[i] + jnp.concatenate(inter[i], axis=0)
        heads.append(_head_out(o, gh[:, sl], og[rs, sl]))
    y_a_rows = [jnp.concatenate(heads[s * N_HEADS:(s + 1) * N_HEADS], axis=1) for s in range(nseq)]
    y_a_rows.append(jnp.concatenate(
        [_head_out(o_dec[hd], gh[:, hd * HEAD_DIM:(hd + 1) * HEAD_DIM], og[prow:, hd * HEAD_DIM:(hd + 1) * HEAD_DIM])
         for hd in range(N_HEADS)], axis=1))

    conv_rows = []
    for s in range(nseq):
        rs = slice(s * rows, (s + 1) * rows)
        us = u[rs]
        prev = cv_scr[s]
        u_m1 = jnp.where(rid == 0, prev[7:8, :], pltpu.roll(us, 1, 0))
        u_m2 = jnp.where(rid == 0, prev[6:7, :], jnp.where(rid == 1, prev[7:8, :], pltpu.roll(us, 2, 0)))
        conv_rows.append(cw[0:1, :] * u_m2 + cw[1:2, :] * u_m1 + cw[2:3, :] * us)
        cv_scr[s] = us[rows - 8:rows, :]
    conv_d, cod_ref[0] = _decode_conv(u[prow:], c0_ref[0], cw, DEC_SEQS, dec_len)
    conv_rows.append(conv_d)

    y_a_in = jnp.concatenate(y_a_rows, axis=0)
    z = bg * jnp.concatenate(conv_rows, axis=0)
    out = _merge_out(x, y_a_in, z, gates, wa_ref, wb_ref, wo_ref)
    xo_ref[...] = out[:prow].reshape(nseq, rows, D_MODEL)
    xod_ref[...] = out[prow:]

    @pl.when(t == pl.num_programs(1) - 1)
    def _():
        for s in range(nseq):
            for hd in range(N_HEADS):
                so_ref[0, s, hd] = st_scr[s, hd].T
            co_ref[0, s] = u[(s + 1) * rows - 2:(s + 1) * rows, :]


def _mix(x, xd, s0, c0, lbl, gm, win, cw, gh, wa, wb, wo):
    bsz, seq, _ = x.shape
    dbsz, dec_len, _ = xd.shape
    nseq, rows = MIX_SEQS, MIX_ROWS
    tiles = seq // rows
    assert seq % rows == 0 and bsz % nseq == 0
    assert dbsz == DEC_SEQS * (bsz // nseq) * tiles, "decode sequences are spread evenly over the grid steps"
    assert dec_len % 8 == 0 and dec_len >= CONV_W - 1
    n_in = win.shape[1]
    drows = DEC_SEQS * dec_len
    x_spec = pl.BlockSpec((nseq, rows, D_MODEL), lambda b, t: (b, t, 0))
    xd_spec = pl.BlockSpec((drows, D_MODEL), lambda b, t: (b * tiles + t, 0))
    sd_spec = pl.BlockSpec((1, DEC_SEQS, N_HEADS, HEAD_DIM, HEAD_DIM), lambda b, t: (0, b * tiles + t, 0, 0, 0))
    cd_spec = pl.BlockSpec((1, DEC_SEQS, CONV_W - 1, D_B), lambda b, t: (0, b * tiles + t, 0, 0))
    xo, so, co, xod, sod, cod = pl.pallas_call(
        functools.partial(_mix_kernel, dec_len=dec_len),
        grid=(bsz // nseq, tiles),
        in_specs=[x_spec, xd_spec, sd_spec, cd_spec, _const_spec(lbl.shape), _const_spec((1, D_MODEL)),
                  _const_spec((D_MODEL, n_in)), _const_spec((CONV_W, D_B)), _const_spec((1, D_A)),
                  _const_spec((D_A, D_MODEL)), _const_spec((D_B, D_MODEL)), _const_spec((D_MODEL, D_MODEL))],
        out_specs=[x_spec,
                   pl.BlockSpec((1, nseq, N_HEADS, HEAD_DIM, HEAD_DIM), lambda b, t: (0, b, 0, 0, 0)),
                   pl.BlockSpec((1, nseq, CONV_W - 1, D_B), lambda b, t: (0, b, 0, 0)),
                   xd_spec, sd_spec, cd_spec],
        out_shape=[jax.ShapeDtypeStruct(x.shape, F32),
                   jax.ShapeDtypeStruct((1, bsz, N_HEADS, HEAD_DIM, HEAD_DIM), F32),
                   jax.ShapeDtypeStruct((1, bsz, CONV_W - 1, D_B), F32),
                   jax.ShapeDtypeStruct((dbsz * dec_len, D_MODEL), F32),
                   jax.ShapeDtypeStruct(s0.shape, F32), jax.ShapeDtypeStruct(c0.shape, F32)],
        scratch_shapes=[pltpu.VMEM((nseq, N_HEADS, HEAD_DIM, HEAD_DIM), F32), pltpu.VMEM((nseq, 8, D_B), F32)],
        compiler_params=pltpu.CompilerParams(dimension_semantics=("arbitrary", "arbitrary"),
                                             vmem_limit_bytes=VMEM_LIMIT_BYTES),
        name="mix",
    )(x, xd.reshape(dbsz * dec_len, D_MODEL), s0, c0, lbl, gm, win, cw, gh, wa, wb, wo)
    return xo, so, co, xod.reshape(xd.shape), sod, cod


def kernel(x_prompt, x_sample, state_hgrn, state_conv, lower_bound_logits, g_ffn1, w1_ffn1, w3_ffn1, w2_ffn1,
           g_mix, w_in, conv_w, g_hgrn_out, w_a_out, w_b_out, w_o, g_ffn2, w1_ffn2, w3_ffn2, w2_ffn2, g_final):
    assert state_hgrn.shape[0] == 1, "single-layer trunk"
    sq = lambda w: w.reshape(w.shape[1:])
    ffn1 = (g_ffn1, sq(w1_ffn1), sq(w3_ffn1), sq(w2_ffn1))
    ffn2 = (g_ffn2, sq(w1_ffn2), sq(w3_ffn2), sq(w2_ffn2))
    gfin = g_final.reshape(1, D_MODEL)
    mixw = (lower_bound_logits, g_mix, sq(w_in), sq(conv_w), g_hgrn_out, sq(w_a_out), sq(w_b_out), sq(w_o))

    def ffn(xa, xb, weights, final_norm):
        ya, yb = _ffn(xa.reshape(-1, D_MODEL), xb.reshape(-1, D_MODEL), *weights, gfin, final_norm=final_norm)
        return ya.reshape(xa.shape), yb.reshape(xb.shape)

    xp, xs = ffn(x_prompt, x_sample, ffn1, False)
    xp, sh_p, sc_p, xs, sh_s, sc_s = _mix(xp, xs, state_hgrn, state_conv, *mixw)
    yp, ys = ffn(xp, xs, ffn2, True)
    return yp, ys, sh_p, sc_p, sh_s, sc_s
```

```python
import functools

import jax
import jax.numpy as jnp
from jax import lax
from jax.experimental import pallas as pl
from jax.experimental.pallas import tpu as pltpu

F32 = jnp.float32
BF16 = jnp.bfloat16

D_MODEL = 1024
D_A = 512
HEAD_DIM = 128
N_HEADS = 4
D_B = 512
CONV_W = 3
D_FF = 2816
CHUNK = 32
EPS = 1e-6

VMEM_LIMIT_BYTES = 56 * 1024 * 1024

FFN_ROWS = 512
FFN_COLS = 256
FFN_STAGE_SLOTS = 2
MIX_ROWS = 256
MIX_SEQS = 2
DEC_SEQS = 4
GATE_COLS = 256


def _dot(a, b):
    return jnp.dot(a, b, preferred_element_type=F32)


def _dot_nt(a, b):
    return lax.dot_general(a, b, (((1,), (1,)), ((), ())), preferred_element_type=F32)


def _rmsnorm(x, g):
    return x * lax.rsqrt(jnp.mean(x * x, axis=-1, keepdims=True) + EPS) * g


def _norm_rows(x, g):
    r = lax.rsqrt(jnp.mean(x * x, axis=-1, keepdims=True) + EPS)
    xg = x * g
    return (xg * r).astype(BF16), xg.astype(BF16), r


def _const_spec(shape):
    nd = len(shape)
    return pl.BlockSpec(shape, lambda *_: (0,) * nd, pipeline_mode=pl.Buffered(1))


def _ffn_rows(x_ref, o_ref, g_ref, w1_ref, w3_ref, w2_ref, gf_ref, final_norm, before_chunk=None):
    x = x_ref[...]
    h, h0, r = _norm_rows(x, g_ref[...])
    acc = None
    for c in range(D_FF // FFN_COLS):
        sl = slice(c * FFN_COLS, (c + 1) * FFN_COLS)
        if before_chunk is not None:
            before_chunk(c)
        if c == 0:
            a = r * _dot(h0, w1_ref[:, sl])
            b = r * _dot(h0, w3_ref[:, sl])
        else:
            a = _dot(h, w1_ref[:, sl])
            b = _dot(h, w3_ref[:, sl])
        act = (a * jax.nn.sigmoid(a) * b).astype(BF16)
        part = _dot(act, w2_ref[sl, :])
        if before_chunk is None:
            acc = part if acc is None else acc + part
        elif c == 0:
            o_ref[...] = part
        else:
            o_ref[...] += part
    y = x_ref[...] + 0.5 * (o_ref[...] if acc is None else acc)
    if final_norm:
        y = _rmsnorm(y, gf_ref[...])
    o_ref[...] = y


def _ffn_chunk_copies(hbm, stage, sem, c):
    (w1_hbm, w3_hbm, w2_hbm), (s1_ref, s3_ref, s2_ref) = hbm, stage
    sl = pl.ds(c * FFN_COLS, FFN_COLS)
    slot = c % FFN_STAGE_SLOTS
    return (pltpu.make_async_copy(w1_hbm.at[:, sl], s1_ref.at[slot], sem.at[slot, 0]),
            pltpu.make_async_copy(w3_hbm.at[:, sl], s3_ref.at[slot], sem.at[slot, 1]),
            pltpu.make_async_copy(w2_hbm.at[sl, :], s2_ref.at[slot], sem.at[slot, 2]))


def _ffn_kernel(xa_ref, xb_ref, g_ref, w1_hbm, w3_hbm, w2_hbm, gf_ref, oa_ref, ob_ref,
                w1_ref, w3_ref, w2_ref, s1_ref, s3_ref, s2_ref, sem, *, steps_a, final_norm):
    i = pl.program_id(0)
    n_chunks = D_FF // FFN_COLS
    weights = (w1_ref, w3_ref, w2_ref)
    stage = (s1_ref, s3_ref, s2_ref)
    copies = functools.partial(_ffn_chunk_copies, (w1_hbm, w3_hbm, w2_hbm), stage, sem)

    @pl.when(i == 0)
    def _():
        for c in range(min(FFN_STAGE_SLOTS, n_chunks)):
            for cp in copies(c):
                cp.start()

        def arrive(c):
            sl = slice(c * FFN_COLS, (c + 1) * FFN_COLS)
            slot = c % FFN_STAGE_SLOTS
            for cp in copies(c):
                cp.wait()
            w1_ref[:, sl] = s1_ref[slot].astype(BF16)
            w3_ref[:, sl] = s3_ref[slot].astype(BF16)
            w2_ref[sl, :] = s2_ref[slot].astype(BF16)
            if c + FFN_STAGE_SLOTS < n_chunks:
                for cp in copies(c + FFN_STAGE_SLOTS):
                    cp.start()

        _ffn_rows(xa_ref, oa_ref, g_ref, *weights, gf_ref, final_norm, before_chunk=arrive)

    @pl.when((i > 0) & (i < steps_a))
    def _():
        _ffn_rows(xa_ref, oa_ref, g_ref, *weights, gf_ref, final_norm)

    @pl.when(i >= steps_a)
    def _():
        _ffn_rows(xb_ref, ob_ref, g_ref, *weights, gf_ref, final_norm)


def _ffn(xa, xb, g, w1, w3, w2, g_final, *, final_norm):
    rows = FFN_ROWS
    na, nb = xa.shape[0], xb.shape[0]
    assert na % rows == 0 and nb % rows == 0
    steps_a, steps_b = na // rows, nb // rows
    assert steps_a >= 1
    a_spec = pl.BlockSpec((rows, D_MODEL), lambda i: (jnp.minimum(i, steps_a - 1), 0))
    b_spec = pl.BlockSpec((rows, D_MODEL), lambda i: (jnp.maximum(i - steps_a, 0), 0))
    hbm_spec = pl.BlockSpec(memory_space=pl.ANY)
    slots = FFN_STAGE_SLOTS
    return pl.pallas_call(
        functools.partial(_ffn_kernel, steps_a=steps_a, final_norm=final_norm),
        grid=(steps_a + steps_b,),
        in_specs=[a_spec, b_spec, _const_spec((1, D_MODEL)), hbm_spec, hbm_spec, hbm_spec,
                  _const_spec((1, D_MODEL))],
        out_specs=[a_spec, b_spec],
        out_shape=[jax.ShapeDtypeStruct((na, D_MODEL), F32), jax.ShapeDtypeStruct((nb, D_MODEL), F32)],
        scratch_shapes=[pltpu.VMEM((D_MODEL, D_FF), BF16), pltpu.VMEM((D_MODEL, D_FF), BF16),
                        pltpu.VMEM((D_FF, D_MODEL), BF16),
                        pltpu.VMEM((slots, D_MODEL, FFN_COLS), F32), pltpu.VMEM((slots, D_MODEL, FFN_COLS), F32),
                        pltpu.VMEM((slots, FFN_COLS, D_MODEL), F32), pltpu.SemaphoreType.DMA((slots, 3))],
        compiler_params=pltpu.CompilerParams(dimension_semantics=("arbitrary",),
                                             vmem_limit_bytes=VMEM_LIMIT_BYTES),
        name="ffn_final" if final_norm else "ffn",
    )(xa, xb, g, w1, w3, w2, g_final)


def _lower_bound(lbl):
    m = jnp.max(lbl, axis=0, keepdims=True)
    e = jnp.exp(lbl - m)
    return e[0:1, :] / jnp.sum(e, axis=0, keepdims=True)


def _causal_in_chunk(rows, chunk, key_major=False):
    shift = chunk.bit_length() - 1
    assert 1 << shift == chunk
    r = lax.broadcasted_iota(jnp.int32, (rows, rows), 1 if key_major else 0)
    c = lax.broadcasted_iota(jnp.int32, (rows, rows), 0 if key_major else 1)
    return (lax.shift_right_logical(r, shift) == lax.shift_right_logical(c, shift)) & (c <= r)


def _chunk_cumsum(x, chunk):
    pos = lax.broadcasted_iota(jnp.int32, x.shape, 0) & (chunk - 1)
    step = 1
    while step < chunk:
        x = x + jnp.where(pos >= step, pltpu.roll(x, step, 0), 0.0)
        step *= 2
    return x


def _chunk_last(x, chunk):
    rows, width = x.shape
    return jnp.concatenate(
        [jnp.broadcast_to(x[n * chunk + chunk - 1:(n + 1) * chunk, :], (chunk, width))
         for n in range(rows // chunk)], axis=0)


def _gate_products(h, h0, r, win_ref):
    pf = r * _dot(h0, win_ref[:, 1 * D_A:2 * D_A].astype(BF16))
    pq = _dot(h, win_ref[:, 0 * D_A:1 * D_A].astype(BF16))
    pv = _dot(h, win_ref[:, 2 * D_A:3 * D_A].astype(BF16))
    og = _dot(h, win_ref[:, 3 * D_A:4 * D_A].astype(BF16))
    return pq, pf, pv, og


def _decay_chain(pq, pf, lb, chunk):
    one_m = 1.0 - lb
    sig = jax.nn.sigmoid(pf)
    f = lb + one_m * sig
    k_in = one_m * (1.0 - sig)
    b = _chunk_cumsum(jnp.log(f), chunk)
    blast = _chunk_last(b, chunk)
    q_d = pq * jnp.exp(b)
    k_d = k_in * jnp.exp(-b)
    k_w = k_in * jnp.exp(blast - b)
    return q_d, k_d, k_w, blast


def _intra(q_b, k_d_h, v_h, tril):
    scores = jnp.where(tril, _dot_nt(q_b, k_d_h.astype(BF16)), 0.0)
    return _dot(scores.astype(BF16), v_h.astype(BF16))


def _head_out(o, gh_h, og_h):
    o = o * lax.rsqrt(jnp.mean(o * o, axis=-1, keepdims=True) + EPS)
    return o * gh_h * (og_h * jax.nn.sigmoid(og_h))


def _chunk_blocks(x_h, rows, chunk):
    shift = chunk.bit_length() - 1
    rid = lax.shift_right_logical(lax.broadcasted_iota(jnp.int32, (rows, HEAD_DIM), 0), shift)
    xb = x_h.astype(BF16)
    zero = jnp.zeros_like(xb)
    return jnp.concatenate([jnp.where(rid == n, xb, zero) for n in range(rows // chunk)], axis=1)


def _conv_inputs(h, win_ref):
    bg = _dot(h, win_ref[:, 4 * D_A:4 * D_A + D_B].astype(BF16))
    cg = _dot(h, win_ref[:, 4 * D_A + D_B:4 * D_A + 2 * D_B].astype(BF16))
    vv = _dot(h, win_ref[:, 4 * D_A + 2 * D_B:4 * D_A + 3 * D_B].astype(BF16))
    return bg, cg * vv


def _gate_piece(h, win_ref, j):
    lo = 4 * D_A + 3 * D_B + j * GATE_COLS
    return jax.nn.sigmoid(_dot(h, win_ref[:, lo:lo + GATE_COLS].astype(BF16)))


def _merge_out(x, y_a_in, z, gates, wa_ref, wb_ref, wo_ref):
    half = D_MODEL // GATE_COLS
    y_a = _dot(y_a_in.astype(BF16), wa_ref[...].astype(BF16))
    y_b = _dot(z.astype(BF16), wb_ref[...].astype(BF16))
    merged = jnp.concatenate(gates[:half], axis=1) * y_a + jnp.concatenate(gates[half:], axis=1) * y_b
    return x + _dot(merged.astype(BF16), wo_ref[...].astype(BF16))


def _decode_head(hd, q_d, k_d, k_w, v, blast, tril, s0_ref, so_ref, nseq, chunk):
    rows = nseq * chunk
    sl = slice(hd * HEAD_DIM, (hd + 1) * HEAD_DIM)
    q_b = q_d[:, sl].astype(BF16)
    o_intra = _intra(q_b, k_d[:, sl], v[:, sl], tril)
    ds = _dot(k_w[:, sl].T.astype(BF16), _chunk_blocks(v[:, sl], rows, chunk))
    decay_t = jnp.exp(blast[:, sl]).T
    inter = []
    for j in range(nseq):
        s0 = s0_ref[0, j, hd]
        inter.append(_dot(q_b[j * chunk:(j + 1) * chunk], s0.astype(BF16)))
        so_ref[0, j, hd] = (s0 * decay_t[:, j * chunk:j * chunk + 1]
                            + ds[:, j * HEAD_DIM:(j + 1) * HEAD_DIM])
    return o_intra + jnp.concatenate(inter, axis=0)


def _decode_conv(u, c0, cw, nseq, chunk):
    tid = lax.broadcasted_iota(jnp.int32, (nseq, chunk, D_B), 1)
    u3 = u.reshape(nseq, chunk, D_B)
    r1 = pltpu.roll(u, 1, 0).reshape(nseq, chunk, D_B)
    r2 = pltpu.roll(u, 2, 0).reshape(nseq, chunk, D_B)
    u_m1 = jnp.where(tid == 0, c0[:, 1:2, :], r1)
    u_m2 = jnp.where(tid == 0, c0[:, 0:1, :], jnp.where(tid == 1, c0[:, 1:2, :], r2))
    conv = cw[0:1, :] * u_m2 + cw[1:2, :] * u_m1 + cw[2:3, :] * u3
    return conv.reshape(nseq * chunk, D_B), u3[:, chunk - 2:chunk, :]


def _mix_kernel(x_ref, xd_ref, s0_ref, c0_ref, lbl_ref, gm_ref, win_ref, cw_ref, gh_ref, wa_ref, wb_ref, wo_ref,
                xo_ref, so_ref, co_ref, xod_ref, sod_ref, cod_ref, st_scr, cv_scr, *, dec_len):
    nseq, rows, chunk = MIX_SEQS, MIX_ROWS, CHUNK
    prow = nseq * rows
    t = pl.program_id(1)

    @pl.when(t == 0)
    def _():
        st_scr[...] = jnp.zeros_like(st_scr)
        cv_scr[...] = jnp.zeros_like(cv_scr)

    x = jnp.concatenate([x_ref[...].reshape(prow, D_MODEL), xd_ref[...]], axis=0)
    h, h0, r = _norm_rows(x, gm_ref[...])
    lb = _lower_bound(lbl_ref[...])
    pq, pf, v, og = _gate_products(h, h0, r, win_ref)
    q_d, k_d, k_w, blast = _decay_chain(pq[:prow], pf[:prow], lb, chunk)
    qd_d, kd_d, kw_d, blast_d = _decay_chain(pq[prow:], pf[prow:], lb, dec_len)
    triu = _causal_in_chunk(rows, chunk, key_major=True)
    gh = gh_ref[...]
    cw = cw_ref[...]
    rid = lax.broadcasted_iota(jnp.int32, (rows, D_B), 0)

    units = [(s, hd, slice(s * rows, (s + 1) * rows), slice(hd * HEAD_DIM, (hd + 1) * HEAD_DIM))
             for s in range(nseq) for hd in range(N_HEADS)]
    q_b = [q_d[rs, sl].astype(BF16) for _, _, rs, sl in units]
    v_t = [v[rs, sl].T.astype(BF16) for _, _, rs, sl in units]
    scores_t = [_dot_nt(k_d[rs, sl].astype(BF16), q_b[i]) for i, (_, _, rs, sl) in enumerate(units)]
    ds_t = [_dot(v_t[i], _chunk_blocks(k_w[rs, sl], rows, chunk)) for i, (_, _, rs, sl) in enumerate(units)]
    o_intra = [_dot(v_t[i], jnp.where(triu, scores_t[i], 0.0).astype(BF16)).T for i in range(len(units))]
    s_t = [st_scr[s, hd] for s, hd, _, _ in units]
    inter = [[] for _ in units]
    assert 2 * D_MODEL // GATE_COLS == rows // chunk
    gates = []
    tril_d = _causal_in_chunk(DEC_SEQS * dec_len, dec_len)
    o_dec = []
    for n in range(rows // chunk):
        gates.append(_gate_piece(h, win_ref, n))
        if n < N_HEADS:
            o_dec.append(_decode_head(n, qd_d, kd_d, kw_d, v[prow:], blast_d, tril_d, s0_ref, sod_ref,
                                      DEC_SEQS, dec_len))
        for i, (_, _, rs, sl) in enumerate(units):
            inter[i].append(_dot(q_b[i][n * chunk:(n + 1) * chunk], s_t[i].T.astype(BF16)))
            decay = jnp.exp(blast[rs, sl][n * chunk:n * chunk + 1, :])
            s_t[i] = s_t[i] * decay + ds_t[i][:, n * HEAD_DIM:(n + 1) * HEAD_DIM]
    assert len(o_dec) == N_HEADS
    heads = []
    for i, (s, hd, rs, sl) in enumerate(units):
        st_scr[s, hd] = s_t[i]
        o = o_intra[i] + jnp.concatenate(inter[i], axis=0)
        heads.append(_head_out(o, gh[:, sl], og[rs, sl]))
    y_a_rows = [jnp.concatenate(heads[s * N_HEADS:(s + 1) * N_HEADS], axis=1) for s in range(nseq)]
    y_a_rows.append(jnp.concatenate(
        [_head_out(o_dec[hd], gh[:, hd * HEAD_DIM:(hd + 1) * HEAD_DIM], og[prow:, hd * HEAD_DIM:(hd + 1) * HEAD_DIM])
         for hd in range(N_HEADS)], axis=1))

    bg, u = _conv_inputs(h, win_ref)
    conv_rows = []
    for s in range(nseq):
        rs = slice(s * rows, (s + 1) * rows)
        us = u[rs]
        prev = cv_scr[s]
        u_m1 = jnp.where(rid == 0, prev[7:8, :], pltpu.roll(us, 1, 0))
        u_m2 = jnp.where(rid == 0, prev[6:7, :], jnp.where(rid == 1, prev[7:8, :], pltpu.roll(us, 2, 0)))
        conv_rows.append(cw[0:1, :] * u_m2 + cw[1:2, :] * u_m1 + cw[2:3, :] * us)
        cv_scr[s] = us[rows - 8:rows, :]
    conv_d, cod_ref[0] = _decode_conv(u[prow:], c0_ref[0], cw, DEC_SEQS, dec_len)
    conv_rows.append(conv_d)

    y_a_in = jnp.concatenate(y_a_rows, axis=0)
    z = bg * jnp.concatenate(conv_rows, axis=0)
    out = _merge_out(x, y_a_in, z, gates, wa_ref, wb_ref, wo_ref)
    xo_ref[...] = out[:prow].reshape(nseq, rows, D_MODEL)
    xod_ref[...] = out[prow:]

    @pl.when(t == pl.num_programs(1) - 1)
    def _():
        for s in range(nseq):
            for hd in range(N_HEADS):
                so_ref[0, s, hd] = st_scr[s, hd].T
            co_ref[0, s] = u[(s + 1) * rows - 2:(s + 1) * rows, :]


def _mix(x, xd, s0, c0, lbl, gm, win, cw, gh, wa, wb, wo):
    bsz, seq, _ = x.shape
    dbsz, dec_len, _ = xd.shape
    nseq, rows = MIX_SEQS, MIX_ROWS
    tiles = seq // rows
    assert seq % rows == 0 and bsz % nseq == 0
    assert dbsz == DEC_SEQS * (bsz // nseq) * tiles, "decode sequences are spread evenly over the grid steps"
    assert dec_len % 8 == 0 and dec_len >= CONV_W - 1
    n_in = win.shape[1]
    drows = DEC_SEQS * dec_len
    x_spec = pl.BlockSpec((nseq, rows, D_MODEL), lambda b, t: (b, t, 0))
    xd_spec = pl.BlockSpec((drows, D_MODEL), lambda b, t: (b * tiles + t, 0))
    sd_spec = pl.BlockSpec((1, DEC_SEQS, N_HEADS, HEAD_DIM, HEAD_DIM), lambda b, t: (0, b * tiles + t, 0, 0, 0))
    cd_spec = pl.BlockSpec((1, DEC_SEQS, CONV_W - 1, D_B), lambda b, t: (0, b * tiles + t, 0, 0))
    xo, so, co, xod, sod, cod = pl.pallas_call(
        functools.partial(_mix_kernel, dec_len=dec_len),
        grid=(bsz // nseq, tiles),
        in_specs=[x_spec, xd_spec, sd_spec, cd_spec, _const_spec(lbl.shape), _const_spec((1, D_MODEL)),
                  _const_spec((D_MODEL, n_in)), _const_spec((CONV_W, D_B)), _const_spec((1, D_A)),
                  _const_spec((D_A, D_MODEL)), _const_spec((D_B, D_MODEL)), _const_spec((D_MODEL, D_MODEL))],
        out_specs=[x_spec,
                   pl.BlockSpec((1, nseq, N_HEADS, HEAD_DIM, HEAD_DIM), lambda b, t: (0, b, 0, 0, 0)),
                   pl.BlockSpec((1, nseq, CONV_W - 1, D_B), lambda b, t: (0, b, 0, 0)),
                   xd_spec, sd_spec, cd_spec],
        out_shape=[jax.ShapeDtypeStruct(x.shape, F32),
                   jax.ShapeDtypeStruct((1, bsz, N_HEADS, HEAD_DIM, HEAD_DIM), F32),
                   jax.ShapeDtypeStruct((1, bsz, CONV_W - 1, D_B), F32),
                   jax.ShapeDtypeStruct((dbsz * dec_len, D_MODEL), F32),
                   jax.ShapeDtypeStruct(s0.shape, F32), jax.ShapeDtypeStruct(c0.shape, F32)],
        scratch_shapes=[pltpu.VMEM((nseq, N_HEADS, HEAD_DIM, HEAD_DIM), F32), pltpu.VMEM((nseq, 8, D_B), F32)],
        compiler_params=pltpu.CompilerParams(dimension_semantics=("arbitrary", "arbitrary"),
                                             vmem_limit_bytes=VMEM_LIMIT_BYTES),
        name="mix",
    )(x, xd.reshape(dbsz * dec_len, D_MODEL), s0, c0, lbl, gm, win, cw, gh, wa, wb, wo)
    return xo, so, co, xod.reshape(xd.shape), sod, cod


def kernel(x_prompt, x_sample, state_hgrn, state_conv, lower_bound_logits, g_ffn1, w1_ffn1, w3_ffn1, w2_ffn1,
           g_mix, w_in, conv_w, g_hgrn_out, w_a_out, w_b_out, w_o, g_ffn2, w1_ffn2, w3_ffn2, w2_ffn2, g_final):
    assert state_hgrn.shape[0] == 1, "single-layer trunk"
    sq = lambda w: w.reshape(w.shape[1:])
    ffn1 = (g_ffn1, sq(w1_ffn1), sq(w3_ffn1), sq(w2_ffn1))
    ffn2 = (g_ffn2, sq(w1_ffn2), sq(w3_ffn2), sq(w2_ffn2))
    gfin = g_final.reshape(1, D_MODEL)
    mixw = (lower_bound_logits, g_mix, sq(w_in), sq(conv_w), g_hgrn_out, sq(w_a_out), sq(w_b_out), sq(w_o))

    def ffn(xa, xb, weights, final_norm):
        ya, yb = _ffn(xa.reshape(-1, D_MODEL), xb.reshape(-1, D_MODEL), *weights, gfin, final_norm=final_norm)
        return ya.reshape(xa.shape), yb.reshape(xb.shape)

    xp, xs = ffn(x_prompt, x_sample, ffn1, False)
    xp, sh_p, sc_p, xs, sh_s, sc_s = _mix(xp, xs, state_hgrn, state_conv, *mixw)
    yp, ys = ffn(xp, xs, ffn2, True)
    return yp, ys, sh_p, sc_p, sh_s, sc_s
```

```python
import functools

import jax
import jax.numpy as jnp
from jax import lax
from jax.experimental import pallas as pl
from jax.experimental.pallas import tpu as pltpu

F32 = jnp.float32
BF16 = jnp.bfloat16

D_MODEL = 1024
D_A = 512
HEAD_DIM = 128
N_HEADS = 4
D_B = 512
CONV_W = 3
D_FF = 2816
CHUNK = 32
EPS = 1e-6

VMEM_LIMIT_BYTES = 117 * 512 * 1024

FFN_ROWS = 512
FFN_COLS = 256
FFN_STAGE_SLOTS = 2
MIX_ROWS = 256
MIX_SEQS = 2
DEC_SEQS = 4
GATE_COLS = 256


def _dot(a, b):
    return jnp.dot(a, b, preferred_element_type=F32)


def _dot_nt(a, b):
    return lax.dot_general(a, b, (((1,), (1,)), ((), ())), preferred_element_type=F32)


def _rmsnorm(x, g):
    return x * lax.rsqrt(jnp.mean(x * x, axis=-1, keepdims=True) + EPS) * g


def _norm_rows(x, g):
    r = lax.rsqrt(jnp.mean(x * x, axis=-1, keepdims=True) + EPS)
    xg = x * g
    return (xg * r).astype(BF16), xg.astype(BF16), r


def _const_spec(shape):
    nd = len(shape)
    return pl.BlockSpec(shape, lambda *_: (0,) * nd, pipeline_mode=pl.Buffered(1))


def _ffn_rows(x_ref, o_ref, g_ref, w1_ref, w3_ref, w2_ref, gf_ref, final_norm, before_chunk=None):
    x = x_ref[...]
    h, h0, r = _norm_rows(x, g_ref[...])
    acc = None
    for c in range(D_FF // FFN_COLS):
        sl = slice(c * FFN_COLS, (c + 1) * FFN_COLS)
        if before_chunk is not None:
            before_chunk(c)
        if c == 0:
            a = r * _dot(h0, w1_ref[:, sl])
            b = r * _dot(h0, w3_ref[:, sl])
        else:
            a = _dot(h, w1_ref[:, sl])
            b = _dot(h, w3_ref[:, sl])
        act = (a * jax.nn.sigmoid(a) * b).astype(BF16)
        part = _dot(act, w2_ref[sl, :])
        if before_chunk is None:
            acc = part if acc is None else acc + part
        elif c == 0:
            o_ref[...] = part
        else:
            o_ref[...] += part
    y = x_ref[...] + 0.5 * (o_ref[...] if acc is None else acc)
    if final_norm:
        y = _rmsnorm(y, gf_ref[...])
    o_ref[...] = y


def _ffn_chunk_copies(hbm, stage, sem, c):
    (w1_hbm, w3_hbm, w2_hbm), (s1_ref, s3_ref, s2_ref) = hbm, stage
    sl = pl.ds(c * FFN_COLS, FFN_COLS)
    slot = c % FFN_STAGE_SLOTS
    return (pltpu.make_async_copy(w1_hbm.at[:, sl], s1_ref.at[slot], sem.at[slot, 0]),
            pltpu.make_async_copy(w3_hbm.at[:, sl], s3_ref.at[slot], sem.at[slot, 1]),
            pltpu.make_async_copy(w2_hbm.at[sl, :], s2_ref.at[slot], sem.at[slot, 2]))


def _ffn_kernel(xa_ref, xb_ref, g_ref, w1_hbm, w3_hbm, w2_hbm, gf_ref, oa_ref, ob_ref,
                w1_ref, w3_ref, w2_ref, s1_ref, s3_ref, s2_ref, sem, *, steps_a, final_norm):
    i = pl.program_id(0)
    n_chunks = D_FF // FFN_COLS
    weights = (w1_ref, w3_ref, w2_ref)
    stage = (s1_ref, s3_ref, s2_ref)
    copies = functools.partial(_ffn_chunk_copies, (w1_hbm, w3_hbm, w2_hbm), stage, sem)

    @pl.when(i == 0)
    def _():
        for c in range(min(FFN_STAGE_SLOTS, n_chunks)):
            for cp in copies(c):
                cp.start()

        def arrive(c):
            sl = slice(c * FFN_COLS, (c + 1) * FFN_COLS)
            slot = c % FFN_STAGE_SLOTS
            for cp in copies(c):
                cp.wait()
            w1_ref[:, sl] = s1_ref[slot].astype(BF16)
            w3_ref[:, sl] = s3_ref[slot].astype(BF16)
            w2_ref[sl, :] = s2_ref[slot].astype(BF16)
            if c + FFN_STAGE_SLOTS < n_chunks:
                for cp in copies(c + FFN_STAGE_SLOTS):
                    cp.start()

        _ffn_rows(xa_ref, oa_ref, g_ref, *weights, gf_ref, final_norm, before_chunk=arrive)

    @pl.when((i > 0) & (i < steps_a))
    def _():
        _ffn_rows(xa_ref, oa_ref, g_ref, *weights, gf_ref, final_norm)

    @pl.when(i >= steps_a)
    def _():
        _ffn_rows(xb_ref, ob_ref, g_ref, *weights, gf_ref, final_norm)


def _ffn(xa, xb, g, w1, w3, w2, g_final, *, final_norm):
    rows = FFN_ROWS
    na, nb = xa.shape[0], xb.shape[0]
    assert na % rows == 0 and nb % rows == 0
    steps_a, steps_b = na // rows, nb // rows
    assert steps_a >= 1
    a_spec = pl.BlockSpec((rows, D_MODEL), lambda i: (jnp.minimum(i, steps_a - 1), 0))
    b_spec = pl.BlockSpec((rows, D_MODEL), lambda i: (jnp.maximum(i - steps_a, 0), 0))
    hbm_spec = pl.BlockSpec(memory_space=pl.ANY)
    slots = FFN_STAGE_SLOTS
    return pl.pallas_call(
        functools.partial(_ffn_kernel, steps_a=steps_a, final_norm=final_norm),
        grid=(steps_a + steps_b,),
        in_specs=[a_spec, b_spec, _const_spec((1, D_MODEL)), hbm_spec, hbm_spec, hbm_spec,
                  _const_spec((1, D_MODEL))],
        out_specs=[a_spec, b_spec],
        out_shape=[jax.ShapeDtypeStruct((na, D_MODEL), F32), jax.ShapeDtypeStruct((nb, D_MODEL), F32)],
        scratch_shapes=[pltpu.VMEM((D_MODEL, D_FF), BF16), pltpu.VMEM((D_MODEL, D_FF), BF16),
                        pltpu.VMEM((D_FF, D_MODEL), BF16),
                        pltpu.VMEM((slots, D_MODEL, FFN_COLS), F32), pltpu.VMEM((slots, D_MODEL, FFN_COLS), F32),
                        pltpu.VMEM((slots, FFN_COLS, D_MODEL), F32), pltpu.SemaphoreType.DMA((slots, 3))],
        compiler_params=pltpu.CompilerParams(dimension_semantics=("arbitrary",),
                                             vmem_limit_bytes=VMEM_LIMIT_BYTES),
        name="ffn_final" if final_norm else "ffn",
    )(xa, xb, g, w1, w3, w2, g_final)


def _lower_bound(lbl):
    m = jnp.max(lbl, axis=0, keepdims=True)
    e = jnp.exp(lbl - m)
    return e[0:1, :] / jnp.sum(e, axis=0, keepdims=True)


def _causal_in_chunk(rows, chunk, key_major=False):
    shift = chunk.bit_length() - 1
    assert 1 << shift == chunk
    r = lax.broadcasted_iota(jnp.int32, (rows, rows), 1 if key_major else 0)
    c = lax.broadcasted_iota(jnp.int32, (rows, rows), 0 if key_major else 1)
    return (lax.shift_right_logical(r, shift) == lax.shift_right_logical(c, shift)) & (c <= r)


def _chunk_cumsum(x, chunk):
    pos = lax.broadcasted_iota(jnp.int32, x.shape, 0) & (chunk - 1)
    step = 1
    while step < chunk:
        x = x + jnp.where(pos >= step, pltpu.roll(x, step, 0), 0.0)
        step *= 2
    return x


def _chunk_last(x, chunk):
    rows, width = x.shape
    return jnp.concatenate(
        [jnp.broadcast_to(x[n * chunk + chunk - 1:(n + 1) * chunk, :], (chunk, width))
         for n in range(rows // chunk)], axis=0)


WIN_GROUPS = ((0, 4), (4, 7), (7, 11))


class _MixWeights:
    def __init__(self, win_ref, wa_ref, wb_ref, wo_ref, copies=None):
        self.win_ref, self.wa_ref, self.wb_ref, self.wo_ref = win_ref, wa_ref, wb_ref, wo_ref
        self.copies = copies
        self.waited = set()

    def _arrive(self, key):
        if self.copies is not None and key not in self.waited:
            for cp in self.copies[key]:
                cp.wait()
            self.waited.add(key)

    def win(self, lo, width):
        (g,) = [i for i, (a, b) in enumerate(WIN_GROUPS) if a * D_A <= lo and lo + width <= b * D_A]
        self._arrive(g)
        return self.win_ref[:, lo:lo + width].astype(BF16)

    def wa(self):
        self._arrive("out")
        return self.wa_ref[...].astype(BF16)

    def wb(self):
        self._arrive("out")
        return self.wb_ref[...].astype(BF16)

    def wo(self):
        self._arrive("out")
        return self.wo_ref[...].astype(BF16)

    def all_arrived(self):
        return self.copies is None or self.waited == set(self.copies)


def _mix_weight_copies(hbm, vmem, sem):
    win_hbm, wa_hbm, wb_hbm, wo_hbm = hbm
    win_ref, wa_ref, wb_ref, wo_ref = vmem
    copies = {}
    for g, (a, b) in enumerate(WIN_GROUPS):
        sl = pl.ds(a * D_A, (b - a) * D_A)
        copies[g] = [pltpu.make_async_copy(win_hbm.at[:, sl], win_ref.at[:, sl], sem.at[g])]
    n = len(WIN_GROUPS)
    copies["out"] = [pltpu.make_async_copy(wa_hbm, wa_ref, sem.at[n]),
                     pltpu.make_async_copy(wb_hbm, wb_ref, sem.at[n + 1]),
                     pltpu.make_async_copy(wo_hbm, wo_ref, sem.at[n + 2])]
    return copies


def _gate_products(h, h0, r, w):
    pf = r * _dot(h0, w.win(1 * D_A, D_A))
    pq = _dot(h, w.win(0 * D_A, D_A))
    pv = _dot(h, w.win(2 * D_A, D_A))
    og = _dot(h, w.win(3 * D_A, D_A))
    return pq, pf, pv, og


def _decay_chain(pq, pf, lb, chunk):
    one_m = 1.0 - lb
    sig = jax.nn.sigmoid(pf)
    f = lb + one_m * sig
    k_in = one_m * (1.0 - sig)
    b = _chunk_cumsum(jnp.log(f), chunk)
    blast = _chunk_last(b, chunk)
    q_d = pq * jnp.exp(b)
    k_d = k_in * jnp.exp(-b)
    k_w = k_in * jnp.exp(blast - b)
    return q_d, k_d, k_w, blast


def _intra(q_b, k_d_h, v_h, tril):
    scores = jnp.where(tril, _dot_nt(q_b, k_d_h.astype(BF16)), 0.0)
    return _dot(scores.astype(BF16), v_h.astype(BF16))


def _head_out(o, gh_h, og_h):
    o = o * lax.rsqrt(jnp.mean(o * o, axis=-1, keepdims=True) + EPS)
    return o * gh_h * (og_h * jax.nn.sigmoid(og_h))


def _chunk_blocks(x_h, rows, chunk):
    shift = chunk.bit_length() - 1
    rid = lax.shift_right_logical(lax.broadcasted_iota(jnp.int32, (rows, HEAD_DIM), 0), shift)
    xb = x_h.astype(BF16)
    zero = jnp.zeros_like(xb)
    return jnp.concatenate([jnp.where(rid == n, xb, zero) for n in range(rows // chunk)], axis=1)


def _conv_inputs(h, w):
    bg = _dot(h, w.win(4 * D_A, D_B))
    cg = _dot(h, w.win(4 * D_A + D_B, D_B))
    vv = _dot(h, w.win(4 * D_A + 2 * D_B, D_B))
    return bg, cg * vv


def _gate_piece(h, w, j):
    lo = 4 * D_A + 3 * D_B + j * GATE_COLS
    return jax.nn.sigmoid(_dot(h, w.win(lo, GATE_COLS)))


def _merge_out(x, y_a_in, z, gates, w):
    half = D_MODEL // GATE_COLS
    y_a = _dot(y_a_in.astype(BF16), w.wa())
    y_b = _dot(z.astype(BF16), w.wb())
    merged = jnp.concatenate(gates[:half], axis=1) * y_a + jnp.concatenate(gates[half:], axis=1) * y_b
    return x + _dot(merged.astype(BF16), w.wo())


def _decode_head(hd, q_d, k_d, k_w, v, blast, tril, s0_ref, so_ref, nseq, chunk):
    rows = nseq * chunk
    sl = slice(hd * HEAD_DIM, (hd + 1) * HEAD_DIM)
    q_b = q_d[:, sl].astype(BF16)
    o_intra = _intra(q_b, k_d[:, sl], v[:, sl], tril)
    ds = _dot(k_w[:, sl].T.astype(BF16), _chunk_blocks(v[:, sl], rows, chunk))
    decay_t = jnp.exp(blast[:, sl]).T
    inter = []
    for j in range(nseq):
        s0 = s0_ref[0, j, hd]
        inter.append(_dot(q_b[j * chunk:(j + 1) * chunk], s0.astype(BF16)))
        so_ref[0, j, hd] = (s0 * decay_t[:, j * chunk:j * chunk + 1]
                            + ds[:, j * HEAD_DIM:(j + 1) * HEAD_DIM])
    return o_intra + jnp.concatenate(inter, axis=0)


def _decode_conv(u, c0, cw, nseq, chunk):
    tid = lax.broadcasted_iota(jnp.int32, (nseq, chunk, D_B), 1)
    u3 = u.reshape(nseq, chunk, D_B)
    r1 = pltpu.roll(u, 1, 0).reshape(nseq, chunk, D_B)
    r2 = pltpu.roll(u, 2, 0).reshape(nseq, chunk, D_B)
    u_m1 = jnp.where(tid == 0, c0[:, 1:2, :], r1)
    u_m2 = jnp.where(tid == 0, c0[:, 0:1, :], jnp.where(tid == 1, c0[:, 1:2, :], r2))
    conv = cw[0:1, :] * u_m2 + cw[1:2, :] * u_m1 + cw[2:3, :] * u3
    return conv.reshape(nseq * chunk, D_B), u3[:, chunk - 2:chunk, :]


def _mix_kernel(x_ref, xd_ref, s0_ref, c0_ref, lbl_ref, gm_ref, win_hbm, cw_ref, gh_ref, wa_hbm, wb_hbm, wo_hbm,
                xo_ref, so_ref, co_ref, xod_ref, sod_ref, cod_ref, st_scr, cv_scr,
                win_ref, wa_ref, wb_ref, wo_ref, sem, *, dec_len):
    t = pl.program_id(1)
    first = (pl.program_id(0) == 0) & (t == 0)
    vmem = (win_ref, wa_ref, wb_ref, wo_ref)
    refs = (x_ref, xd_ref, s0_ref, c0_ref, lbl_ref, gm_ref, cw_ref, gh_ref,
            xo_ref, so_ref, co_ref, xod_ref, sod_ref, cod_ref, st_scr, cv_scr)

    @pl.when(t == 0)
    def _():
        st_scr[...] = jnp.zeros_like(st_scr)
        cv_scr[...] = jnp.zeros_like(cv_scr)

    @pl.when(first)
    def _():
        copies = _mix_weight_copies((win_hbm, wa_hbm, wb_hbm, wo_hbm), vmem, sem)
        for group in copies.values():
            for cp in group:
                cp.start()
        w = _MixWeights(*vmem, copies=copies)
        _mix_step(w, *refs, dec_len=dec_len)
        assert w.all_arrived()

    @pl.when(jnp.logical_not(first))
    def _():
        _mix_step(_MixWeights(*vmem), *refs, dec_len=dec_len)


def _mix_step(w, x_ref, xd_ref, s0_ref, c0_ref, lbl_ref, gm_ref, cw_ref, gh_ref,
              xo_ref, so_ref, co_ref, xod_ref, sod_ref, cod_ref, st_scr, cv_scr, *, dec_len):
    nseq, rows, chunk = MIX_SEQS, MIX_ROWS, CHUNK
    prow = nseq * rows
    t = pl.program_id(1)

    x = jnp.concatenate([x_ref[...].reshape(prow, D_MODEL), xd_ref[...]], axis=0)
    h, h0, r = _norm_rows(x, gm_ref[...])
    lb = _lower_bound(lbl_ref[...])
    pq, pf, v, og = _gate_products(h, h0, r, w)
    q_d, k_d, k_w, blast = _decay_chain(pq[:prow], pf[:prow], lb, chunk)
    qd_d, kd_d, kw_d, blast_d = _decay_chain(pq[prow:], pf[prow:], lb, dec_len)
    triu = _causal_in_chunk(rows, chunk, key_major=True)
    gh = gh_ref[...]
    bg, u = _conv_inputs(h, w)
    cw = cw_ref[...]
    rid = lax.broadcasted_iota(jnp.int32, (rows, D_B), 0)

    units = [(s, hd, slice(s * rows, (s + 1) * rows), slice(hd * HEAD_DIM, (hd + 1) * HEAD_DIM))
             for s in range(nseq) for hd in range(N_HEADS)]
    q_b = [q_d[rs, sl].astype(BF16) for _, _, rs, sl in units]
    v_t = [v[rs, sl].T.astype(BF16) for _, _, rs, sl in units]
    scores_t = [_dot_nt(k_d[rs, sl].astype(BF16), q_b[i]) for i, (_, _, rs, sl) in enumerate(units)]
    ds_t = [_dot(v_t[i], _chunk_blocks(k_w[rs, sl], rows, chunk)) for i, (_, _, rs, sl) in enumerate(units)]
    o_intra = [_dot(v_t[i], jnp.where(triu, scores_t[i], 0.0).astype(BF16)).T for i in range(len(units))]
    s_t = [st_scr[s, hd] for s, hd, _, _ in units]
    inter = [[] for _ in units]
    assert 2 * D_MODEL // GATE_COLS == rows // chunk
    gates = []
    tril_d = _causal_in_chunk(DEC_SEQS * dec_len, dec_len)
    o_dec = []
    for n in range(rows // chunk):
        gates.append(_gate_piece(h, w, n))
        if n < N_HEADS:
            o_dec.append(_decode_head(n, qd_d, kd_d, kw_d, v[prow:], blast_d, tril_d, s0_ref, sod_ref,
                                      DEC_SEQS, dec_len))
        for i, (_, _, rs, sl) in enumerate(units):
            inter[i].append(_dot(q_b[i][n * chunk:(n + 1) * chunk], s_t[i].T.astype(BF16)))
            decay = jnp.exp(blast[rs, sl][n * chunk:n * chunk + 1, :])
            s_t[i] = s_t[i] * decay + ds_t[i][:, n * HEAD_DIM:(n + 1) * HEAD_DIM]
    assert len(o_dec) == N_HEADS
    heads = []
    for i, (s, hd, rs, sl) in enumerate(units):
        st_scr[s, hd] = s_t[i]
        o = o_intra[i] + jnp.concatenate(inter[i], axis=0)
        heads.append(_head_out(o, gh[:, sl], og[rs, sl]))
    y_a_rows = [jnp.concatenate(heads[s * N_HEADS:(s + 1) * N_HEADS], axis=1) for s in range(nseq)]
    y_a_rows.append(jnp.concatenate(
        [_head_out(o_dec[hd], gh[:, hd * HEAD_DIM:(hd + 1) * HEAD_DIM], og[prow:, hd * HEAD_DIM:(hd + 1) * HEAD_DIM])
         for hd in range(N_HEADS)], axis=1))

    conv_rows = []
    for s in range(nseq):
        rs = slice(s * rows, (s + 1) * rows)
        us = u[rs]
        prev = cv_scr[s]
        u_m1 = jnp.where(rid == 0, prev[7:8, :], pltpu.roll(us, 1, 0))
        u_m2 = jnp.where(rid == 0, prev[6:7, :], jnp.where(rid == 1, prev[7:8, :], pltpu.roll(us, 2, 0)))
        conv_rows.append(cw[0:1, :] * u_m2 + cw[1:2, :] * u_m1 + cw[2:3, :] * us)
        cv_scr[s] = us[rows - 8:rows, :]
    conv_d, cod_ref[0] = _decode_conv(u[prow:], c0_ref[0], cw, DEC_SEQS, dec_len)
    conv_rows.append(conv_d)

    y_a_in = jnp.concatenate(y_a_rows, axis=0)
    z = bg * jnp.concatenate(conv_rows, axis=0)
    out = _merge_out(x, y_a_in, z, gates, w)
    xo_ref[...] = out[:prow].reshape(nseq, rows, D_MODEL)
    xod_ref[...] = out[prow:]

    @pl.when(t == pl.num_programs(1) - 1)
    def _():
        for s in range(nseq):
            for hd in range(N_HEADS):
                so_ref[0, s, hd] = st_scr[s, hd].T
            co_ref[0, s] = u[(s + 1) * rows - 2:(s + 1) * rows, :]


def _mix(x, xd, s0, c0, lbl, gm, win, cw, gh, wa, wb, wo):
    bsz, seq, _ = x.shape
    dbsz, dec_len, _ = xd.shape
    nseq, rows = MIX_SEQS, MIX_ROWS
    tiles = seq // rows
    assert seq % rows == 0 and bsz % nseq == 0
    assert dbsz == DEC_SEQS * (bsz // nseq) * tiles, "decode sequences are spread evenly over the grid steps"
    assert dec_len % 8 == 0 and dec_len >= CONV_W - 1
    n_in = win.shape[1]
    drows = DEC_SEQS * dec_len
    x_spec = pl.BlockSpec((nseq, rows, D_MODEL), lambda b, t: (b, t, 0))
    xd_spec = pl.BlockSpec((drows, D_MODEL), lambda b, t: (b * tiles + t, 0))
    sd_spec = pl.BlockSpec((1, DEC_SEQS, N_HEADS, HEAD_DIM, HEAD_DIM), lambda b, t: (0, b * tiles + t, 0, 0, 0))
    cd_spec = pl.BlockSpec((1, DEC_SEQS, CONV_W - 1, D_B), lambda b, t: (0, b * tiles + t, 0, 0))
    hbm_spec = pl.BlockSpec(memory_space=pl.ANY)
    xo, so, co, xod, sod, cod = pl.pallas_call(
        functools.partial(_mix_kernel, dec_len=dec_len),
        grid=(bsz // nseq, tiles),
        in_specs=[x_spec, xd_spec, sd_spec, cd_spec, _const_spec(lbl.shape), _const_spec((1, D_MODEL)),
                  hbm_spec, _const_spec((CONV_W, D_B)), _const_spec((1, D_A)), hbm_spec, hbm_spec, hbm_spec],
        out_specs=[x_spec,
                   pl.BlockSpec((1, nseq, N_HEADS, HEAD_DIM, HEAD_DIM), lambda b, t: (0, b, 0, 0, 0)),
                   pl.BlockSpec((1, nseq, CONV_W - 1, D_B), lambda b, t: (0, b, 0, 0)),
                   xd_spec, sd_spec, cd_spec],
        out_shape=[jax.ShapeDtypeStruct(x.shape, F32),
                   jax.ShapeDtypeStruct((1, bsz, N_HEADS, HEAD_DIM, HEAD_DIM), F32),
                   jax.ShapeDtypeStruct((1, bsz, CONV_W - 1, D_B), F32),
                   jax.ShapeDtypeStruct((dbsz * dec_len, D_MODEL), F32),
                   jax.ShapeDtypeStruct(s0.shape, F32), jax.ShapeDtypeStruct(c0.shape, F32)],
        scratch_shapes=[pltpu.VMEM((nseq, N_HEADS, HEAD_DIM, HEAD_DIM), F32), pltpu.VMEM((nseq, 8, D_B), F32),
                        pltpu.VMEM(win.shape, F32), pltpu.VMEM(wa.shape, F32), pltpu.VMEM(wb.shape, F32),
                        pltpu.VMEM(wo.shape, F32), pltpu.SemaphoreType.DMA((len(WIN_GROUPS) + 3,))],
        compiler_params=pltpu.CompilerParams(dimension_semantics=("arbitrary", "arbitrary"),
                                             vmem_limit_bytes=VMEM_LIMIT_BYTES),
        name="mix",
    )(x, xd.reshape(dbsz * dec_len, D_MODEL), s0, c0, lbl, gm, win, cw, gh, wa, wb, wo)
    return xo, so, co, xod.reshape(xd.shape), sod, cod


def kernel(x_prompt, x_sample, state_hgrn, state_conv, lower_bound_logits, g_ffn1, w1_ffn1, w3_ffn1, w2_ffn1,
           g_mix, w_in, conv_w, g_hgrn_out, w_a_out, w_b_out, w_o, g_ffn2, w1_ffn2, w3_ffn2, w2_ffn2, g_final):
    assert state_hgrn.shape[0] == 1, "single-layer trunk"
    sq = lambda w: w.reshape(w.shape[1:])
    ffn1 = (g_ffn1, sq(w1_ffn1), sq(w3_ffn1), sq(w2_ffn1))
    ffn2 = (g_ffn2, sq(w1_ffn2), sq(w3_ffn2), sq(w2_ffn2))
    gfin = g_final.reshape(1, D_MODEL)
    mixw = (lower_bound_logits, g_mix, sq(w_in), sq(conv_w), g_hgrn_out, sq(w_a_out), sq(w_b_out), sq(w_o))

    def ffn(xa, xb, weights, final_norm):
        ya, yb = _ffn(xa.reshape(-1, D_MODEL), xb.reshape(-1, D_MODEL), *weights, gfin, final_norm=final_norm)
        return ya.reshape(xa.shape), yb.reshape(xb.shape)

    xp, xs = ffn(x_prompt, x_sample, ffn1, False)
    xp, sh_p, sc_p, xs, sh_s, sc_s = _mix(xp, xs, state_hgrn, state_conv, *mixw)
    yp, ys = ffn(xp, xs, ffn2, True)
    return yp, ys, sh_p, sc_p, sh_s, sc_s
```

```python
import functools

import jax
import jax.numpy as jnp
from jax import lax
from jax.experimental import pallas as pl
from jax.experimental.pallas import tpu as pltpu

F32 = jnp.float32
BF16 = jnp.bfloat16

D_MODEL = 1024
D_A = 512
HEAD_DIM = 128
N_HEADS = 4
D_B = 512
CONV_W = 3
D_FF = 2816
CHUNK = 32
EPS = 1e-6

VMEM_LIMIT_BYTES = 56 * 1024 * 1024

FFN_ROWS = 512
FFN_COLS = 256
FFN_STAGE_SLOTS = 2
MIX_ROWS = 256
MIX_SEQS = 2
DEC_SEQS = 4
GATE_COLS = 256


def _dot(a, b):
    return jnp.dot(a, b, preferred_element_type=F32)


def _dot_nt(a, b):
    return lax.dot_general(a, b, (((1,), (1,)), ((), ())), preferred_element_type=F32)


def _rmsnorm(x, g):
    return x * lax.rsqrt(jnp.mean(x * x, axis=-1, keepdims=True) + EPS) * g


def _norm_rows(x, g):
    r = lax.rsqrt(jnp.mean(x * x, axis=-1, keepdims=True) + EPS)
    xg = x * g
    return (xg * r).astype(BF16), xg.astype(BF16), r


def _const_spec(shape):
    nd = len(shape)
    return pl.BlockSpec(shape, lambda *_: (0,) * nd, pipeline_mode=pl.Buffered(1))


def _ffn_rows(x_ref, o_ref, g_ref, w1_ref, w3_ref, w2_ref, gf_ref, final_norm, before_chunk=None):
    x = x_ref[...]
    h, h0, r = _norm_rows(x, g_ref[...])
    acc = None
    for c in range(D_FF // FFN_COLS):
        sl = slice(c * FFN_COLS, (c + 1) * FFN_COLS)
        if before_chunk is not None:
            before_chunk(c)
        if c == 0:
            a = r * _dot(h0, w1_ref[:, sl])
            b = r * _dot(h0, w3_ref[:, sl])
        else:
            a = _dot(h, w1_ref[:, sl])
            b = _dot(h, w3_ref[:, sl])
        act = (a * jax.nn.sigmoid(a) * b).astype(BF16)
        part = _dot(act, w2_ref[sl, :])
        if before_chunk is None:
            acc = part if acc is None else acc + part
        elif c == 0:
            o_ref[...] = part
        else:
            o_ref[...] += part
    y = x_ref[...] + 0.5 * (o_ref[...] if acc is None else acc)
    if final_norm:
        y = _rmsnorm(y, gf_ref[...])
    o_ref[...] = y


def _ffn_chunk_copies(hbm, stage, sem, c):
    (w1_hbm, w3_hbm, w2_hbm), (s1_ref, s3_ref, s2_ref) = hbm, stage
    sl = pl.ds(c * FFN_COLS, FFN_COLS)
    slot = c % FFN_STAGE_SLOTS
    return (pltpu.make_async_copy(w1_hbm.at[:, sl], s1_ref.at[slot], sem.at[slot, 0]),
            pltpu.make_async_copy(w3_hbm.at[:, sl], s3_ref.at[slot], sem.at[slot, 1]),
            pltpu.make_async_copy(w2_hbm.at[sl, :], s2_ref.at[slot], sem.at[slot, 2]))


def _ffn_kernel(xa_ref, xb_ref, g_ref, w1_hbm, w3_hbm, w2_hbm, gf_ref, oa_ref, ob_ref,
                w1_ref, w3_ref, w2_ref, s1_ref, s3_ref, s2_ref, sem, *, steps_a, final_norm):
    i = pl.program_id(0)
    n_chunks = D_FF // FFN_COLS
    weights = (w1_ref, w3_ref, w2_ref)
    stage = (s1_ref, s3_ref, s2_ref)
    copies = functools.partial(_ffn_chunk_copies, (w1_hbm, w3_hbm, w2_hbm), stage, sem)

    @pl.when(i == 0)
    def _():
        for c in range(min(FFN_STAGE_SLOTS, n_chunks)):
            for cp in copies(c):
                cp.start()

        def arrive(c):
            sl = slice(c * FFN_COLS, (c + 1) * FFN_COLS)
            slot = c % FFN_STAGE_SLOTS
            for cp in copies(c):
                cp.wait()
            w1_ref[:, sl] = s1_ref[slot].astype(BF16)
            w3_ref[:, sl] = s3_ref[slot].astype(BF16)
            w2_ref[sl, :] = s2_ref[slot].astype(BF16)
            if c + FFN_STAGE_SLOTS < n_chunks:
                for cp in copies(c + FFN_STAGE_SLOTS):
                    cp.start()

        _ffn_rows(xa_ref, oa_ref, g_ref, *weights, gf_ref, final_norm, before_chunk=arrive)

    @pl.when((i > 0) & (i < steps_a))
    def _():
        _ffn_rows(xa_ref, oa_ref, g_ref, *weights, gf_ref, final_norm)

    @pl.when(i >= steps_a)
    def _():
        _ffn_rows(xb_ref, ob_ref, g_ref, *weights, gf_ref, final_norm)


def _ffn(xa, xb, g, w1, w3, w2, g_final, *, final_norm):
    rows = FFN_ROWS
    na, nb = xa.shape[0], xb.shape[0]
    assert na % rows == 0 and nb % rows == 0
    steps_a, steps_b = na // rows, nb // rows
    assert steps_a >= 1
    a_spec = pl.BlockSpec((rows, D_MODEL), lambda i: (jnp.minimum(i, steps_a - 1), 0))
    b_spec = pl.BlockSpec((rows, D_MODEL), lambda i: (jnp.maximum(i - steps_a, 0), 0))
    hbm_spec = pl.BlockSpec(memory_space=pl.ANY)
    slots = FFN_STAGE_SLOTS
    return pl.pallas_call(
        functools.partial(_ffn_kernel, steps_a=steps_a, final_norm=final_norm),
        grid=(steps_a + steps_b,),
        in_specs=[a_spec, b_spec, _const_spec((1, D_MODEL)), hbm_spec, hbm_spec, hbm_spec,
                  _const_spec((1, D_MODEL))],
        out_specs=[a_spec, b_spec],
        out_shape=[jax.ShapeDtypeStruct((na, D_MODEL), F32), jax.ShapeDtypeStruct((nb, D_MODEL), F32)],
        scratch_shapes=[pltpu.VMEM((D_MODEL, D_FF), BF16), pltpu.VMEM((D_MODEL, D_FF), BF16),
                        pltpu.VMEM((D_FF, D_MODEL), BF16),
                        pltpu.VMEM((slots, D_MODEL, FFN_COLS), F32), pltpu.VMEM((slots, D_MODEL, FFN_COLS), F32),
                        pltpu.VMEM((slots, FFN_COLS, D_MODEL), F32), pltpu.SemaphoreType.DMA((slots, 3))],
        compiler_params=pltpu.CompilerParams(dimension_semantics=("arbitrary",),
                                             vmem_limit_bytes=VMEM_LIMIT_BYTES),
        name="ffn_final" if final_norm else "ffn",
    )(xa, xb, g, w1, w3, w2, g_final)


def _lower_bound(lbl):
    m = jnp.max(lbl, axis=0, keepdims=True)
    e = jnp.exp(lbl - m)
    return e[0:1, :] / jnp.sum(e, axis=0, keepdims=True)


def _causal_in_chunk(rows, chunk, key_major=False):
    shift = chunk.bit_length() - 1
    assert 1 << shift == chunk
    r = lax.broadcasted_iota(jnp.int32, (rows, rows), 1 if key_major else 0)
    c = lax.broadcasted_iota(jnp.int32, (rows, rows), 0 if key_major else 1)
    return (lax.shift_right_logical(r, shift) == lax.shift_right_logical(c, shift)) & (c <= r)


def _chunk_cumsum(x, chunk):
    pos = lax.broadcasted_iota(jnp.int32, x.shape, 0) & (chunk - 1)
    step = 1
    while step < chunk:
        x = x + jnp.where(pos >= step, pltpu.roll(x, step, 0), 0.0)
        step *= 2
    return x


def _chunk_last(x, chunk):
    rows, width = x.shape
    return jnp.concatenate(
        [jnp.broadcast_to(x[n * chunk + chunk - 1:(n + 1) * chunk, :], (chunk, width))
         for n in range(rows // chunk)], axis=0)


WIN_ORDER = (1, 0, 2, 3, 4, 5, 6, 7, 8, 9, 10)


class _MixWeights:
    def __init__(self, win_ref, wa_ref, wb_ref, wo_ref, rings=None):
        self.win_ref, self.wa_ref, self.wb_ref, self.wo_ref = win_ref, wa_ref, wb_ref, wo_ref
        self.rings = rings
        self.done = [0] * (len(rings) if rings else 0)
        if rings:
            for ring in rings:
                for _, cp, _, _ in ring[:2]:
                    cp.start()

    def _arrive(self, key):
        if self.rings is None:
            return
        for r, ring in enumerate(self.rings):
            while self.done[r] < len(ring) and ring[self.done[r]][0] == key:
                i = self.done[r]
                _, cp, stage, dst = ring[i]
                cp.wait()
                dst[...] = stage[...].astype(BF16)
                if i + 2 < len(ring):
                    ring[i + 2][1].start()
                self.done[r] = i + 1

    def win(self, lo, width):
        assert lo // D_A == (lo + width - 1) // D_A
        self._arrive(("win", lo // D_A))
        return self.win_ref[:, lo:lo + width]

    def wa(self):
        self._arrive("wa")
        return self.wa_ref[...]

    def wb(self):
        self._arrive("wb")
        return self.wb_ref[...]

    def wo(self):
        self._arrive("wo")
        return self.wo_ref[...]

    def all_arrived(self):
        return self.rings is None or all(d == len(r) for d, r in zip(self.done, self.rings))


def _mix_weight_rings(hbm, vmem, stage, sem):
    win_hbm, wa_hbm, wb_hbm, wo_hbm = hbm
    win_ref, wa_ref, wb_ref, wo_ref = vmem
    stage_in, stage_out = stage
    ring_in = []
    for i, j in enumerate(WIN_ORDER):
        sl = pl.ds(j * D_A, D_A)
        ring_in.append((("win", j), pltpu.make_async_copy(win_hbm.at[:, sl], stage_in.at[i % 2], sem.at[0, i % 2]),
                        stage_in.at[i % 2], win_ref.at[:, sl]))
    half = D_MODEL // 2
    outs = [("wa", wa_hbm, wa_ref), ("wb", wb_hbm, wb_ref),
            ("wo", wo_hbm.at[pl.ds(0, half), :], wo_ref.at[pl.ds(0, half), :]),
            ("wo", wo_hbm.at[pl.ds(half, half), :], wo_ref.at[pl.ds(half, half), :])]
    ring_out = [(key, pltpu.make_async_copy(src, stage_out.at[i % 2], sem.at[1, i % 2]), stage_out.at[i % 2], dst)
                for i, (key, src, dst) in enumerate(outs)]
    return [ring_in, ring_out]


def _gate_products(h, h0, r, w):
    pf = r * _dot(h0, w.win(1 * D_A, D_A))
    pq = _dot(h, w.win(0 * D_A, D_A))
    pv = _dot(h, w.win(2 * D_A, D_A))
    og = _dot(h, w.win(3 * D_A, D_A))
    return pq, pf, pv, og


def _decay_chain(pq, pf, lb, chunk):
    one_m = 1.0 - lb
    sig = jax.nn.sigmoid(pf)
    f = lb + one_m * sig
    k_in = one_m * (1.0 - sig)
    b = _chunk_cumsum(jnp.log(f), chunk)
    blast = _chunk_last(b, chunk)
    q_d = pq * jnp.exp(b)
    k_d = k_in * jnp.exp(-b)
    k_w = k_in * jnp.exp(blast - b)
    return q_d, k_d, k_w, blast


def _intra(q_b, k_d_h, v_h, tril):
    scores = jnp.where(tril, _dot_nt(q_b, k_d_h.astype(BF16)), 0.0)
    return _dot(scores.astype(BF16), v_h.astype(BF16))


def _head_out(o, gh_h, og_h):
    o = o * lax.rsqrt(jnp.mean(o * o, axis=-1, keepdims=True) + EPS)
    return o * gh_h * (og_h * jax.nn.sigmoid(og_h))


def _chunk_blocks(x_h, rows, chunk):
    shift = chunk.bit_length() - 1
    rid = lax.shift_right_logical(lax.broadcasted_iota(jnp.int32, (rows, HEAD_DIM), 0), shift)
    xb = x_h.astype(BF16)
    zero = jnp.zeros_like(xb)
    return jnp.concatenate([jnp.where(rid == n, xb, zero) for n in range(rows // chunk)], axis=1)


def _conv_inputs(h, w):
    bg = _dot(h, w.win(4 * D_A, D_B))
    cg = _dot(h, w.win(4 * D_A + D_B, D_B))
    vv = _dot(h, w.win(4 * D_A + 2 * D_B, D_B))
    return bg, cg * vv


def _gate_piece(h, w, j):
    lo = 4 * D_A + 3 * D_B + j * GATE_COLS
    return jax.nn.sigmoid(_dot(h, w.win(lo, GATE_COLS)))


def _merge_out(x, y_a_in, z, gates, w):
    half = D_MODEL // GATE_COLS
    y_a = _dot(y_a_in.astype(BF16), w.wa())
    y_b = _dot(z.astype(BF16), w.wb())
    merged = jnp.concatenate(gates[:half], axis=1) * y_a + jnp.concatenate(gates[half:], axis=1) * y_b
    return x + _dot(merged.astype(BF16), w.wo())


def _decode_head(hd, q_d, k_d, k_w, v, blast, tril, s0_ref, so_ref, nseq, chunk):
    rows = nseq * chunk
    sl = slice(hd * HEAD_DIM, (hd + 1) * HEAD_DIM)
    q_b = q_d[:, sl].astype(BF16)
    o_intra = _intra(q_b, k_d[:, sl], v[:, sl], tril)
    ds = _dot(k_w[:, sl].T.astype(BF16), _chunk_blocks(v[:, sl], rows, chunk))
    decay_t = jnp.exp(blast[:, sl]).T
    inter = []
    for j in range(nseq):
        s0 = s0_ref[0, j, hd]
        inter.append(_dot(q_b[j * chunk:(j + 1) * chunk], s0.astype(BF16)))
        so_ref[0, j, hd] = (s0 * decay_t[:, j * chunk:j * chunk + 1]
                            + ds[:, j * HEAD_DIM:(j + 1) * HEAD_DIM])
    return o_intra + jnp.concatenate(inter, axis=0)


def _decode_conv(u, c0, cw, nseq, chunk):
    tid = lax.broadcasted_iota(jnp.int32, (nseq, chunk, D_B), 1)
    u3 = u.reshape(nseq, chunk, D_B)
    r1 = pltpu.roll(u, 1, 0).reshape(nseq, chunk, D_B)
    r2 = pltpu.roll(u, 2, 0).reshape(nseq, chunk, D_B)
    u_m1 = jnp.where(tid == 0, c0[:, 1:2, :], r1)
    u_m2 = jnp.where(tid == 0, c0[:, 0:1, :], jnp.where(tid == 1, c0[:, 1:2, :], r2))
    conv = cw[0:1, :] * u_m2 + cw[1:2, :] * u_m1 + cw[2:3, :] * u3
    return conv.reshape(nseq * chunk, D_B), u3[:, chunk - 2:chunk, :]


def _mix_kernel(x_ref, xd_ref, s0_ref, c0_ref, lbl_ref, gm_ref, win_hbm, cw_ref, gh_ref, wa_hbm, wb_hbm, wo_hbm,
                xo_ref, so_ref, co_ref, xod_ref, sod_ref, cod_ref, st_scr, cv_scr,
                win_ref, wa_ref, wb_ref, wo_ref, stage_in, stage_out, sem, *, dec_len):
    t = pl.program_id(1)
    first = (pl.program_id(0) == 0) & (t == 0)
    vmem = (win_ref, wa_ref, wb_ref, wo_ref)
    refs = (x_ref, xd_ref, s0_ref, c0_ref, lbl_ref, gm_ref, cw_ref, gh_ref,
            xo_ref, so_ref, co_ref, xod_ref, sod_ref, cod_ref, st_scr, cv_scr)

    @pl.when(t == 0)
    def _():
        st_scr[...] = jnp.zeros_like(st_scr)
        cv_scr[...] = jnp.zeros_like(cv_scr)

    @pl.when(first)
    def _():
        w = _MixWeights(*vmem, rings=_mix_weight_rings((win_hbm, wa_hbm, wb_hbm, wo_hbm), vmem,
                                                      (stage_in, stage_out), sem))
        _mix_step(w, *refs, dec_len=dec_len)
        assert w.all_arrived()

    @pl.when(jnp.logical_not(first))
    def _():
        _mix_step(_MixWeights(*vmem), *refs, dec_len=dec_len)


def _mix_step(w, x_ref, xd_ref, s0_ref, c0_ref, lbl_ref, gm_ref, cw_ref, gh_ref,
              xo_ref, so_ref, co_ref, xod_ref, sod_ref, cod_ref, st_scr, cv_scr, *, dec_len):
    nseq, rows, chunk = MIX_SEQS, MIX_ROWS, CHUNK
    prow = nseq * rows
    t = pl.program_id(1)

    x = jnp.concatenate([x_ref[...].reshape(prow, D_MODEL), xd_ref[...]], axis=0)
    h, h0, r = _norm_rows(x, gm_ref[...])
    lb = _lower_bound(lbl_ref[...])
    pq, pf, v, og = _gate_products(h, h0, r, w)
    q_d, k_d, k_w, blast = _decay_chain(pq[:prow], pf[:prow], lb, chunk)
    qd_d, kd_d, kw_d, blast_d = _decay_chain(pq[prow:], pf[prow:], lb, dec_len)
    triu = _causal_in_chunk(rows, chunk, key_major=True)
    gh = gh_ref[...]
    bg, u = _conv_inputs(h, w)
    cw = cw_ref[...]
    rid = lax.broadcasted_iota(jnp.int32, (rows, D_B), 0)

    units = [(s, hd, slice(s * rows, (s + 1) * rows), slice(hd * HEAD_DIM, (hd + 1) * HEAD_DIM))
             for s in range(nseq) for hd in range(N_HEADS)]
    q_b = [q_d[rs, sl].astype(BF16) for _, _, rs, sl in units]
    v_t = [v[rs, sl].T.astype(BF16) for _, _, rs, sl in units]
    scores_t = [_dot_nt(k_d[rs, sl].astype(BF16), q_b[i]) for i, (_, _, rs, sl) in enumerate(units)]
    ds_t = [_dot(v_t[i], _chunk_blocks(k_w[rs, sl], rows, chunk)) for i, (_, _, rs, sl) in enumerate(units)]
    o_intra = [_dot(v_t[i], jnp.where(triu, scores_t[i], 0.0).astype(BF16)).T for i in range(len(units))]
    s_t = [st_scr[s, hd] for s, hd, _, _ in units]
    inter = [[] for _ in units]
    assert 2 * D_MODEL // GATE_COLS == rows // chunk
    gates = []
    tril_d = _causal_in_chunk(DEC_SEQS * dec_len, dec_len)
    o_dec = []
    for n in range(rows // chunk):
        gates.append(_gate_piece(h, w, n))
        if n < N_HEADS:
            o_dec.append(_decode_head(n, qd_d, kd_d, kw_d, v[prow:], blast_d, tril_d, s0_ref, sod_ref,
                                      DEC_SEQS, dec_len))
        for i, (_, _, rs, sl) in enumerate(units):
            inter[i].append(_dot(q_b[i][n * chunk:(n + 1) * chunk], s_t[i].T.astype(BF16)))
            decay = jnp.exp(blast[rs, sl][n * chunk:n * chunk + 1, :])
            s_t[i] = s_t[i] * decay + ds_t[i][:, n * HEAD_DIM:(n + 1) * HEAD_DIM]
    assert len(o_dec) == N_HEADS
    heads = []
    for i, (s, hd, rs, sl) in enumerate(units):
        st_scr[s, hd] = s_t[i]
        o = o_intra[i] + jnp.concatenate(inter[i], axis=0)
        heads.append(_head_out(o, gh[:, sl], og[rs, sl]))
    y_a_rows = [jnp.concatenate(heads[s * N_HEADS:(s + 1) * N_HEADS], axis=1) for s in range(nseq)]
    y_a_rows.append(jnp.concatenate(
        [_head_out(o_dec[hd], gh[:, hd * HEAD_DIM:(hd + 1) * HEAD_DIM], og[prow:, hd * HEAD_DIM:(hd + 1) * HEAD_DIM])
         for hd in range(N_HEADS)], axis=1))

    conv_rows = []
    for s in range(nseq):
        rs = slice(s * rows, (s + 1) * rows)
        us = u[rs]
        prev = cv_scr[s]
        u_m1 = jnp.where(rid == 0, prev[7:8, :], pltpu.roll(us, 1, 0))
        u_m2 = jnp.where(rid == 0, prev[6:7, :], jnp.where(rid == 1, prev[7:8, :], pltpu.roll(us, 2, 0)))
        conv_rows.append(cw[0:1, :] * u_m2 + cw[1:2, :] * u_m1 + cw[2:3, :] * us)
        cv_scr[s] = us[rows - 8:rows, :]
    conv_d, cod_ref[0] = _decode_conv(u[prow:], c0_ref[0], cw, DEC_SEQS, dec_len)
    conv_rows.append(conv_d)

    y_a_in = jnp.concatenate(y_a_rows, axis=0)
    z = bg * jnp.concatenate(conv_rows, axis=0)
    out = _merge_out(x, y_a_in, z, gates, w)
    xo_ref[...] = out[:prow].reshape(nseq, rows, D_MODEL)
    xod_ref[...] = out[prow:]

    @pl.when(t == pl.num_programs(1) - 1)
    def _():
        for s in range(nseq):
            for hd in range(N_HEADS):
                so_ref[0, s, hd] = st_scr[s, hd].T
            co_ref[0, s] = u[(s + 1) * rows - 2:(s + 1) * rows, :]


def _mix(x, xd, s0, c0, lbl, gm, win, cw, gh, wa, wb, wo):
    bsz, seq, _ = x.shape
    dbsz, dec_len, _ = xd.shape
    nseq, rows = MIX_SEQS, MIX_ROWS
    tiles = seq // rows
    assert seq % rows == 0 and bsz % nseq == 0
    assert dbsz == DEC_SEQS * (bsz // nseq) * tiles, "decode sequences are spread evenly over the grid steps"
    assert dec_len % 8 == 0 and dec_len >= CONV_W - 1
    n_in = win.shape[1]
    drows = DEC_SEQS * dec_len
    x_spec = pl.BlockSpec((nseq, rows, D_MODEL), lambda b, t: (b, t, 0))
    xd_spec = pl.BlockSpec((drows, D_MODEL), lambda b, t: (b * tiles + t, 0))
    sd_spec = pl.BlockSpec((1, DEC_SEQS, N_HEADS, HEAD_DIM, HEAD_DIM), lambda b, t: (0, b * tiles + t, 0, 0, 0))
    cd_spec = pl.BlockSpec((1, DEC_SEQS, CONV_W - 1, D_B), lambda b, t: (0, b * tiles + t, 0, 0))
    hbm_spec = pl.BlockSpec(memory_space=pl.ANY)
    xo, so, co, xod, sod, cod = pl.pallas_call(
        functools.partial(_mix_kernel, dec_len=dec_len),
        grid=(bsz // nseq, tiles),
        in_specs=[x_spec, xd_spec, sd_spec, cd_spec, _const_spec(lbl.shape), _const_spec((1, D_MODEL)),
                  hbm_spec, _const_spec((CONV_W, D_B)), _const_spec((1, D_A)), hbm_spec, hbm_spec, hbm_spec],
        out_specs=[x_spec,
                   pl.BlockSpec((1, nseq, N_HEADS, HEAD_DIM, HEAD_DIM), lambda b, t: (0, b, 0, 0, 0)),
                   pl.BlockSpec((1, nseq, CONV_W - 1, D_B), lambda b, t: (0, b, 0, 0)),
                   xd_spec, sd_spec, cd_spec],
        out_shape=[jax.ShapeDtypeStruct(x.shape, F32),
                   jax.ShapeDtypeStruct((1, bsz, N_HEADS, HEAD_DIM, HEAD_DIM), F32),
                   jax.ShapeDtypeStruct((1, bsz, CONV_W - 1, D_B), F32),
                   jax.ShapeDtypeStruct((dbsz * dec_len, D_MODEL), F32),
                   jax.ShapeDtypeStruct(s0.shape, F32), jax.ShapeDtypeStruct(c0.shape, F32)],
        scratch_shapes=[pltpu.VMEM((nseq, N_HEADS, HEAD_DIM, HEAD_DIM), F32), pltpu.VMEM((nseq, 8, D_B), F32),
                        pltpu.VMEM(win.shape, BF16), pltpu.VMEM(wa.shape, BF16), pltpu.VMEM(wb.shape, BF16),
                        pltpu.VMEM(wo.shape, BF16), pltpu.VMEM((2, D_MODEL, D_A), F32),
                        pltpu.VMEM((2, D_MODEL // 2, D_MODEL), F32), pltpu.SemaphoreType.DMA((2, 2))],
        compiler_params=pltpu.CompilerParams(dimension_semantics=("arbitrary", "arbitrary"),
                                             vmem_limit_bytes=VMEM_LIMIT_BYTES),
        name="mix",
    )(x, xd.reshape(dbsz * dec_len, D_MODEL), s0, c0, lbl, gm, win, cw, gh, wa, wb, wo)
    return xo, so, co, xod.reshape(xd.shape), sod, cod


def kernel(x_prompt, x_sample, state_hgrn, state_conv, lower_bound_logits, g_ffn1, w1_ffn1, w3_ffn1, w2_ffn1,
           g_mix, w_in, conv_w, g_hgrn_out, w_a_out, w_b_out, w_o, g_ffn2, w1_ffn2, w3_ffn2, w2_ffn2, g_final):
    assert state_hgrn.shape[0] == 1, "single-layer trunk"
    sq = lambda w: w.reshape(w.shape[1:])
    ffn1 = (g_ffn1, sq(w1_ffn1), sq(w3_ffn1), sq(w2_ffn1))
    ffn2 = (g_ffn2, sq(w1_ffn2), sq(w3_ffn2), sq(w2_ffn2))
    gfin = g_final.reshape(1, D_MODEL)
    mixw = (lower_bound_logits, g_mix, sq(w_in), sq(conv_w), g_hgrn_out, sq(w_a_out), sq(w_b_out), sq(w_o))

    def ffn(xa, xb, weights, final_norm):
        ya, yb = _ffn(xa.reshape(-1, D_MODEL), xb.reshape(-1, D_MODEL), *weights, gfin, final_norm=final_norm)
        return ya.reshape(xa.shape), yb.reshape(xb.shape)

    xp, xs = ffn(x_prompt, x_sample, ffn1, False)
    xp, sh_p, sc_p, xs, sh_s, sc_s = _mix(xp, xs, state_hgrn, state_conv, *mixw)
    yp, ys = ffn(xp, xs, ffn2, True)
    return yp, ys, sh_p, sc_p, sh_s, sc_s
```

```python
import functools

import jax
import jax.numpy as jnp
from jax import lax
from jax.experimental import pallas as pl
from jax.experimental.pallas import tpu as pltpu

F32 = jnp.float32
BF16 = jnp.bfloat16

D_MODEL = 1024
D_A = 512
HEAD_DIM = 128
N_HEADS = 4
D_B = 512
CONV_W = 3
D_FF = 2816
CHUNK = 32
EPS = 1e-6

VMEM_LIMIT_BYTES = 56 * 1024 * 1024

FFN_ROWS = 512
FFN_COLS = 256
FFN_STAGE_SLOTS = 2
MIX_ROWS = 256
MIX_SEQS = 2
DEC_SEQS = 4
GATE_COLS = 256


def _dot(a, b):
    return jnp.dot(a, b, preferred_element_type=F32)


def _dot_nt(a, b):
    return lax.dot_general(a, b, (((1,), (1,)), ((), ())), preferred_element_type=F32)


def _rmsnorm(x, g):
    return x * lax.rsqrt(jnp.mean(x * x, axis=-1, keepdims=True) + EPS) * g


def _norm_rows(x, g):
    r = lax.rsqrt(jnp.mean(x * x, axis=-1, keepdims=True) + EPS)
    xg = x * g
    return (xg * r).astype(BF16), xg.astype(BF16), r


def _const_spec(shape):
    nd = len(shape)
    return pl.BlockSpec(shape, lambda *_: (0,) * nd, pipeline_mode=pl.Buffered(1))


def _ffn_rows(x_ref, o_ref, g_ref, w1_ref, w3_ref, w2_ref, gf_ref, final_norm, before_chunk=None):
    x = x_ref[...]
    h, h0, r = _norm_rows(x, g_ref[...])
    acc = None
    for c in range(D_FF // FFN_COLS):
        sl = slice(c * FFN_COLS, (c + 1) * FFN_COLS)
        if before_chunk is not None:
            before_chunk(c)
        w1c, w3c, w2c = (w1_ref[:, sl].astype(BF16), w3_ref[:, sl].astype(BF16), w2_ref[sl, :].astype(BF16))
        if c == 0:
            a = r * _dot(h0, w1c)
            b = r * _dot(h0, w3c)
        else:
            a = _dot(h, w1c)
            b = _dot(h, w3c)
        act = (a * jax.nn.sigmoid(a) * b).astype(BF16)
        part = _dot(act, w2c)
        if before_chunk is None:
            acc = part if acc is None else acc + part
        elif c == 0:
            o_ref[...] = part
        else:
            o_ref[...] += part
    y = x_ref[...] + 0.5 * (o_ref[...] if acc is None else acc)
    if final_norm:
        y = _rmsnorm(y, gf_ref[...])
    o_ref[...] = y


def _ffn_chunk_copies(hbm, stage, sem, c):
    (w1_hbm, w3_hbm, w2_hbm), (s1_ref, s3_ref, s2_ref) = hbm, stage
    sl = pl.ds(c * FFN_COLS, FFN_COLS)
    slot = c % FFN_STAGE_SLOTS
    return (pltpu.make_async_copy(w1_hbm.at[:, sl], s1_ref.at[slot], sem.at[slot, 0]),
            pltpu.make_async_copy(w3_hbm.at[:, sl], s3_ref.at[slot], sem.at[slot, 1]),
            pltpu.make_async_copy(w2_hbm.at[sl, :], s2_ref.at[slot], sem.at[slot, 2]))


def _ffn_kernel(xa_ref, xb_ref, g_ref, w1_hbm, w3_hbm, w2_hbm, gf_ref, oa_ref, ob_ref,
                w1_ref, w3_ref, w2_ref, s1_ref, s3_ref, s2_ref, sem, *, steps_a, final_norm):
    i = pl.program_id(0)
    n_chunks = D_FF // FFN_COLS
    weights = (w1_ref, w3_ref, w2_ref)
    stage = (s1_ref, s3_ref, s2_ref)
    copies = functools.partial(_ffn_chunk_copies, (w1_hbm, w3_hbm, w2_hbm), stage, sem)

    @pl.when(i == 0)
    def _():
        for c in range(min(FFN_STAGE_SLOTS, n_chunks)):
            for cp in copies(c):
                cp.start()

        def arrive(c):
            sl = slice(c * FFN_COLS, (c + 1) * FFN_COLS)
            slot = c % FFN_STAGE_SLOTS
            for cp in copies(c):
                cp.wait()
            w1_ref[:, sl] = s1_ref[slot].astype(BF16)
            w3_ref[:, sl] = s3_ref[slot].astype(BF16)
            w2_ref[sl, :] = s2_ref[slot].astype(BF16)
            if c + FFN_STAGE_SLOTS < n_chunks:
                for cp in copies(c + FFN_STAGE_SLOTS):
                    cp.start()

        _ffn_rows(xa_ref, oa_ref, g_ref, *weights, gf_ref, final_norm, before_chunk=arrive)

    @pl.when((i > 0) & (i < steps_a))
    def _():
        _ffn_rows(xa_ref, oa_ref, g_ref, *weights, gf_ref, final_norm)

    @pl.when(i >= steps_a)
    def _():
        _ffn_rows(xb_ref, ob_ref, g_ref, *weights, gf_ref, final_norm)


def _ffn_kernel_resident(xa_ref, xb_ref, g_ref, w1_ref, w3_ref, w2_ref, gf_ref, oa_ref, ob_ref, *,
                         steps_a, final_norm):
    i = pl.program_id(0)

    @pl.when(i < steps_a)
    def _():
        _ffn_rows(xa_ref, oa_ref, g_ref, w1_ref, w3_ref, w2_ref, gf_ref, final_norm)

    @pl.when(i >= steps_a)
    def _():
        _ffn_rows(xb_ref, ob_ref, g_ref, w1_ref, w3_ref, w2_ref, gf_ref, final_norm)


def _ffn(xa, xb, g, w1, w3, w2, g_final, *, final_norm, staged):
    rows = FFN_ROWS
    na, nb = xa.shape[0], xb.shape[0]
    assert na % rows == 0 and nb % rows == 0
    steps_a, steps_b = na // rows, nb // rows
    assert steps_a >= 1
    a_spec = pl.BlockSpec((rows, D_MODEL), lambda i: (jnp.minimum(i, steps_a - 1), 0))
    b_spec = pl.BlockSpec((rows, D_MODEL), lambda i: (jnp.maximum(i - steps_a, 0), 0))
    if not staged:
        return pl.pallas_call(
            functools.partial(_ffn_kernel_resident, steps_a=steps_a, final_norm=final_norm),
            grid=(steps_a + steps_b,),
            in_specs=[a_spec, b_spec, _const_spec((1, D_MODEL)), _const_spec((D_MODEL, D_FF)),
                      _const_spec((D_MODEL, D_FF)), _const_spec((D_FF, D_MODEL)), _const_spec((1, D_MODEL))],
            out_specs=[a_spec, b_spec],
            out_shape=[jax.ShapeDtypeStruct((na, D_MODEL), F32), jax.ShapeDtypeStruct((nb, D_MODEL), F32)],
            compiler_params=pltpu.CompilerParams(dimension_semantics=("arbitrary",),
                                                 vmem_limit_bytes=VMEM_LIMIT_BYTES),
            name="ffn_final" if final_norm else "ffn",
        )(xa, xb, g, w1, w3, w2, g_final)
    hbm_spec = pl.BlockSpec(memory_space=pl.ANY)
    slots = FFN_STAGE_SLOTS
    return pl.pallas_call(
        functools.partial(_ffn_kernel, steps_a=steps_a, final_norm=final_norm),
        grid=(steps_a + steps_b,),
        in_specs=[a_spec, b_spec, _const_spec((1, D_MODEL)), hbm_spec, hbm_spec, hbm_spec,
                  _const_spec((1, D_MODEL))],
        out_specs=[a_spec, b_spec],
        out_shape=[jax.ShapeDtypeStruct((na, D_MODEL), F32), jax.ShapeDtypeStruct((nb, D_MODEL), F32)],
        scratch_shapes=[pltpu.VMEM((D_MODEL, D_FF), BF16), pltpu.VMEM((D_MODEL, D_FF), BF16),
                        pltpu.VMEM((D_FF, D_MODEL), BF16),
                        pltpu.VMEM((slots, D_MODEL, FFN_COLS), F32), pltpu.VMEM((slots, D_MODEL, FFN_COLS), F32),
                        pltpu.VMEM((slots, FFN_COLS, D_MODEL), F32), pltpu.SemaphoreType.DMA((slots, 3))],
        compiler_params=pltpu.CompilerParams(dimension_semantics=("arbitrary",),
                                             vmem_limit_bytes=VMEM_LIMIT_BYTES),
        name="ffn_final" if final_norm else "ffn",
    )(xa, xb, g, w1, w3, w2, g_final)


def _lower_bound(lbl):
    m = jnp.max(lbl, axis=0, keepdims=True)
    e = jnp.exp(lbl - m)
    return e[0:1, :] / jnp.sum(e, axis=0, keepdims=True)


def _causal_in_chunk(rows, chunk, key_major=False):
    shift = chunk.bit_length() - 1
    assert 1 << shift == chunk
    r = lax.broadcasted_iota(jnp.int32, (rows, rows), 1 if key_major else 0)
    c = lax.broadcasted_iota(jnp.int32, (rows, rows), 0 if key_major else 1)
    return (lax.shift_right_logical(r, shift) == lax.shift_right_logical(c, shift)) & (c <= r)


def _chunk_cumsum(x, chunk):
    pos = lax.broadcasted_iota(jnp.int32, x.shape, 0) & (chunk - 1)
    step = 1
    while step < chunk:
        x = x + jnp.where(pos >= step, pltpu.roll(x, step, 0), 0.0)
        step *= 2
    return x


def _chunk_last(x, chunk):
    rows, width = x.shape
    return jnp.concatenate(
        [jnp.broadcast_to(x[n * chunk + chunk - 1:(n + 1) * chunk, :], (chunk, width))
         for n in range(rows // chunk)], axis=0)


def _gate_products(h, h0, r, win_ref):
    pf = r * _dot(h0, win_ref[:, 1 * D_A:2 * D_A].astype(BF16))
    pq = _dot(h, win_ref[:, 0 * D_A:1 * D_A].astype(BF16))
    pv = _dot(h, win_ref[:, 2 * D_A:3 * D_A].astype(BF16))
    og = _dot(h, win_ref[:, 3 * D_A:4 * D_A].astype(BF16))
    return pq, pf, pv, og


def _decay_chain(pq, pf, lb, chunk):
    one_m = 1.0 - lb
    sig = jax.nn.sigmoid(pf)
    f = lb + one_m * sig
    k_in = one_m * (1.0 - sig)
    b = _chunk_cumsum(jnp.log(f), chunk)
    blast = _chunk_last(b, chunk)
    q_d = pq * jnp.exp(b)
    k_d = k_in * jnp.exp(-b)
    k_w = k_in * jnp.exp(blast - b)
    return q_d, k_d, k_w, blast


def _intra(q_b, k_d_h, v_h, tril):
    scores = jnp.where(tril, _dot_nt(q_b, k_d_h.astype(BF16)), 0.0)
    return _dot(scores.astype(BF16), v_h.astype(BF16))


def _head_out(o, gh_h, og_h):
    o = o * lax.rsqrt(jnp.mean(o * o, axis=-1, keepdims=True) + EPS)
    return o * gh_h * (og_h * jax.nn.sigmoid(og_h))


def _chunk_blocks(x_h, rows, chunk):
    shift = chunk.bit_length() - 1
    rid = lax.shift_right_logical(lax.broadcasted_iota(jnp.int32, (rows, HEAD_DIM), 0), shift)
    xb = x_h.astype(BF16)
    zero = jnp.zeros_like(xb)
    return jnp.concatenate([jnp.where(rid == n, xb, zero) for n in range(rows // chunk)], axis=1)


def _conv_inputs(h, win_ref):
    bg = _dot(h, win_ref[:, 4 * D_A:4 * D_A + D_B].astype(BF16))
    cg = _dot(h, win_ref[:, 4 * D_A + D_B:4 * D_A + 2 * D_B].astype(BF16))
    vv = _dot(h, win_ref[:, 4 * D_A + 2 * D_B:4 * D_A + 3 * D_B].astype(BF16))
    return bg, cg * vv


def _gate_piece(h, win_ref, j):
    lo = 4 * D_A + 3 * D_B + j * GATE_COLS
    return jax.nn.sigmoid(_dot(h, win_ref[:, lo:lo + GATE_COLS].astype(BF16)))


def _merge_out(x, y_a_in, z, gates, wa_ref, wb_ref, wo_ref):
    half = D_MODEL // GATE_COLS
    y_a = _dot(y_a_in.astype(BF16), wa_ref[...].astype(BF16))
    y_b = _dot(z.astype(BF16), wb_ref[...].astype(BF16))
    merged = jnp.concatenate(gates[:half], axis=1) * y_a + jnp.concatenate(gates[half:], axis=1) * y_b
    return x + _dot(merged.astype(BF16), wo_ref[...].astype(BF16))


def _decode_head(hd, q_d, k_d, k_w, v, blast, tril, s0_ref, so_ref, nseq, chunk):
    rows = nseq * chunk
    sl = slice(hd * HEAD_DIM, (hd + 1) * HEAD_DIM)
    q_b = q_d[:, sl].astype(BF16)
    o_intra = _intra(q_b, k_d[:, sl], v[:, sl], tril)
    ds = _dot(k_w[:, sl].T.astype(BF16), _chunk_blocks(v[:, sl], rows, chunk))
    decay_t = jnp.exp(blast[:, sl]).T
    inter = []
    for j in range(nseq):
        s0 = s0_ref[0, j, hd]
        inter.append(_dot(q_b[j * chunk:(j + 1) * chunk], s0.astype(BF16)))
        so_ref[0, j, hd] = (s0 * decay_t[:, j * chunk:j * chunk + 1]
                            + ds[:, j * HEAD_DIM:(j + 1) * HEAD_DIM])
    return o_intra + jnp.concatenate(inter, axis=0)


def _decode_conv(u, c0, cw, nseq, chunk):
    tid = lax.broadcasted_iota(jnp.int32, (nseq, chunk, D_B), 1)
    u3 = u.reshape(nseq, chunk, D_B)
    r1 = pltpu.roll(u, 1, 0).reshape(nseq, chunk, D_B)
    r2 = pltpu.roll(u, 2, 0).reshape(nseq, chunk, D_B)
    u_m1 = jnp.where(tid == 0, c0[:, 1:2, :], r1)
    u_m2 = jnp.where(tid == 0, c0[:, 0:1, :], jnp.where(tid == 1, c0[:, 1:2, :], r2))
    conv = cw[0:1, :] * u_m2 + cw[1:2, :] * u_m1 + cw[2:3, :] * u3
    return conv.reshape(nseq * chunk, D_B), u3[:, chunk - 2:chunk, :]


def _mix_kernel(x_ref, xd_ref, s0_ref, c0_ref, lbl_ref, gm_ref, win_ref, cw_ref, gh_ref, wa_ref, wb_ref, wo_ref,
                xo_ref, so_ref, co_ref, xod_ref, sod_ref, cod_ref, st_scr, cv_scr, *, dec_len):
    nseq, rows, chunk = MIX_SEQS, MIX_ROWS, CHUNK
    prow = nseq * rows
    t = pl.program_id(1)

    @pl.when(t == 0)
    def _():
        st_scr[...] = jnp.zeros_like(st_scr)
        cv_scr[...] = jnp.zeros_like(cv_scr)

    x = jnp.concatenate([x_ref[...].reshape(prow, D_MODEL), xd_ref[...]], axis=0)
    h, h0, r = _norm_rows(x, gm_ref[...])
    lb = _lower_bound(lbl_ref[...])
    pq, pf, v, og = _gate_products(h, h0, r, win_ref)
    q_d, k_d, k_w, blast = _decay_chain(pq[:prow], pf[:prow], lb, chunk)
    qd_d, kd_d, kw_d, blast_d = _decay_chain(pq[prow:], pf[prow:], lb, dec_len)
    triu = _causal_in_chunk(rows, chunk, key_major=True)
    gh = gh_ref[...]
    bg, u = _conv_inputs(h, win_ref)
    cw = cw_ref[...]
    rid = lax.broadcasted_iota(jnp.int32, (rows, D_B), 0)

    units = [(s, hd, slice(s * rows, (s + 1) * rows), slice(hd * HEAD_DIM, (hd + 1) * HEAD_DIM))
             for s in range(nseq) for hd in range(N_HEADS)]
    q_b = [q_d[rs, sl].astype(BF16) for _, _, rs, sl in units]
    v_t = [v[rs, sl].T.astype(BF16) for _, _, rs, sl in units]
    scores_t = [_dot_nt(k_d[rs, sl].astype(BF16), q_b[i]) for i, (_, _, rs, sl) in enumerate(units)]
    ds_t = [_dot(v_t[i], _chunk_blocks(k_w[rs, sl], rows, chunk)) for i, (_, _, rs, sl) in enumerate(units)]
    o_intra = [_dot(v_t[i], jnp.where(triu, scores_t[i], 0.0).astype(BF16)).T for i in range(len(units))]
    s_t = [st_scr[s, hd] for s, hd, _, _ in units]
    inter = [[] for _ in units]
    assert 2 * D_MODEL // GATE_COLS == rows // chunk
    gates = []
    tril_d = _causal_in_chunk(DEC_SEQS * dec_len, dec_len)
    o_dec = []
    for n in range(rows // chunk):
        gates.append(_gate_piece(h, win_ref, n))
        if n < N_HEADS:
            o_dec.append(_decode_head(n, qd_d, kd_d, kw_d, v[prow:], blast_d, tril_d, s0_ref, sod_ref,
                                      DEC_SEQS, dec_len))
        for i, (_, _, rs, sl) in enumerate(units):
            inter[i].append(_dot(q_b[i][n * chunk:(n + 1) * chunk], s_t[i].T.astype(BF16)))
            decay = jnp.exp(blast[rs, sl][n * chunk:n * chunk + 1, :])
            s_t[i] = s_t[i] * decay + ds_t[i][:, n * HEAD_DIM:(n + 1) * HEAD_DIM]
    assert len(o_dec) == N_HEADS
    heads = []
    for i, (s, hd, rs, sl) in enumerate(units):
        st_scr[s, hd] = s_t[i]
        o = o_intra[i] + jnp.concatenate(inter[i], axis=0)
        heads.append(_head_out(o, gh[:, sl], og[rs, sl]))
    y_a_rows = [jnp.concatenate(heads[s * N_HEADS:(s + 1) * N_HEADS], axis=1) for s in range(nseq)]
    y_a_rows.append(jnp.concatenate(
        [_head_out(o_dec[hd], gh[:, hd * HEAD_DIM:(hd + 1) * HEAD_DIM], og[prow:, hd * HEAD_DIM:(hd + 1) * HEAD_DIM])
         for hd in range(N_HEADS)], axis=1))

    conv_rows = []
    for s in range(nseq):
        rs = slice(s * rows, (s + 1) * rows)
        us = u[rs]
        prev = cv_scr[s]
        u_m1 = jnp.where(rid == 0, prev[7:8, :], pltpu.roll(us, 1, 0))
        u_m2 = jnp.where(rid == 0, prev[6:7, :], jnp.where(rid == 1, prev[7:8, :], pltpu.roll(us, 2, 0)))
        conv_rows.append(cw[0:1, :] * u_m2 + cw[1:2, :] * u_m1 + cw[2:3, :] * us)
        cv_scr[s] = us[rows - 8:rows, :]
    conv_d, cod_ref[0] = _decode_conv(u[prow:], c0_ref[0], cw, DEC_SEQS, dec_len)
    conv_rows.append(conv_d)

    y_a_in = jnp.concatenate(y_a_rows, axis=0)
    z = bg * jnp.concatenate(conv_rows, axis=0)
    out = _merge_out(x, y_a_in, z, gates, wa_ref, wb_ref, wo_ref)
    xo_ref[...] = out[:prow].reshape(nseq, rows, D_MODEL)
    xod_ref[...] = out[prow:]

    @pl.when(t == pl.num_programs(1) - 1)
    def _():
        for s in range(nseq):
            for hd in range(N_HEADS):
                so_ref[0, s, hd] = st_scr[s, hd].T
            co_ref[0, s] = u[(s + 1) * rows - 2:(s + 1) * rows, :]


def _mix(x, xd, s0, c0, lbl, gm, win, cw, gh, wa, wb, wo):
    bsz, seq, _ = x.shape
    dbsz, dec_len, _ = xd.shape
    nseq, rows = MIX_SEQS, MIX_ROWS
    tiles = seq // rows
    assert seq % rows == 0 and bsz % nseq == 0
    assert dbsz == DEC_SEQS * (bsz // nseq) * tiles, "decode sequences are spread evenly over the grid steps"
    assert dec_len % 8 == 0 and dec_len >= CONV_W - 1
    n_in = win.shape[1]
    drows = DEC_SEQS * dec_len
    x_spec = pl.BlockSpec((nseq, rows, D_MODEL), lambda b, t: (b, t, 0))
    xd_spec = pl.BlockSpec((drows, D_MODEL), lambda b, t: (b * tiles + t, 0))
    sd_spec = pl.BlockSpec((1, DEC_SEQS, N_HEADS, HEAD_DIM, HEAD_DIM), lambda b, t: (0, b * tiles + t, 0, 0, 0))
    cd_spec = pl.BlockSpec((1, DEC_SEQS, CONV_W - 1, D_B), lambda b, t: (0, b * tiles + t, 0, 0))
    xo, so, co, xod, sod, cod = pl.pallas_call(
        functools.partial(_mix_kernel, dec_len=dec_len),
        grid=(bsz // nseq, tiles),
        in_specs=[x_spec, xd_spec, sd_spec, cd_spec, _const_spec(lbl.shape), _const_spec((1, D_MODEL)),
                  _const_spec((D_MODEL, n_in)), _const_spec((CONV_W, D_B)), _const_spec((1, D_A)),
                  _const_spec((D_A, D_MODEL)), _const_spec((D_B, D_MODEL)), _const_spec((D_MODEL, D_MODEL))],
        out_specs=[x_spec,
                   pl.BlockSpec((1, nseq, N_HEADS, HEAD_DIM, HEAD_DIM), lambda b, t: (0, b, 0, 0, 0)),
                   pl.BlockSpec((1, nseq, CONV_W - 1, D_B), lambda b, t: (0, b, 0, 0)),
                   xd_spec, sd_spec, cd_spec],
        out_shape=[jax.ShapeDtypeStruct(x.shape, F32),
                   jax.ShapeDtypeStruct((1, bsz, N_HEADS, HEAD_DIM, HEAD_DIM), F32),
                   jax.ShapeDtypeStruct((1, bsz, CONV_W - 1, D_B), F32),
                   jax.ShapeDtypeStruct((dbsz * dec_len, D_MODEL), F32),
                   jax.ShapeDtypeStruct(s0.shape, F32), jax.ShapeDtypeStruct(c0.shape, F32)],
        scratch_shapes=[pltpu.VMEM((nseq, N_HEADS, HEAD_DIM, HEAD_DIM), F32), pltpu.VMEM((nseq, 8, D_B), F32)],
        compiler_params=pltpu.CompilerParams(dimension_semantics=("arbitrary", "arbitrary"),
                                             vmem_limit_bytes=VMEM_LIMIT_BYTES),
        name="mix",
    )(x, xd.reshape(dbsz * dec_len, D_MODEL), s0, c0, lbl, gm, win, cw, gh, wa, wb, wo)
    return xo, so, co, xod.reshape(xd.shape), sod, cod


def kernel(x_prompt, x_sample, state_hgrn, state_conv, lower_bound_logits, g_ffn1, w1_ffn1, w3_ffn1, w2_ffn1,
           g_mix, w_in, conv_w, g_hgrn_out, w_a_out, w_b_out, w_o, g_ffn2, w1_ffn2, w3_ffn2, w2_ffn2, g_final):
    assert state_hgrn.shape[0] == 1, "single-layer trunk"
    sq = lambda w: w.reshape(w.shape[1:])
    ffn1 = (g_ffn1, sq(w1_ffn1), sq(w3_ffn1), sq(w2_ffn1))
    ffn2 = (g_ffn2, sq(w1_ffn2), sq(w3_ffn2), sq(w2_ffn2))
    gfin = g_final.reshape(1, D_MODEL)
    mixw = (lower_bound_logits, g_mix, sq(w_in), sq(conv_w), g_hgrn_out, sq(w_a_out), sq(w_b_out), sq(w_o))

    def ffn(xa, xb, weights, final_norm):
        ya, yb = _ffn(xa.reshape(-1, D_MODEL), xb.reshape(-1, D_MODEL), *weights, gfin, final_norm=final_norm,
                      staged=not final_norm)
        return ya.reshape(xa.shape), yb.reshape(xb.shape)

    xp, xs = ffn(x_prompt, x_sample, ffn1, False)
    xp, sh_p, sc_p, xs, sh_s, sc_s = _mix(xp, xs, state_hgrn, state_conv, *mixw)
    yp, ys = ffn(xp, xs, ffn2, True)
    return yp, ys, sh_p, sc_p, sh_s, sc_s
```

```python
import functools

import jax
import jax.numpy as jnp
from jax import lax
from jax.experimental import pallas as pl
from jax.experimental.pallas import tpu as pltpu

F32 = jnp.float32
BF16 = jnp.bfloat16

D_MODEL = 1024
D_A = 512
HEAD_DIM = 128
N_HEADS = 4
D_B = 512
CONV_W = 3
D_FF = 2816
CHUNK = 32
EPS = 1e-6

VMEM_LIMIT_BYTES = 56 * 1024 * 1024

FFN_ROWS = 512
FFN_COLS = 256
FFN_STAGE_SLOTS = 2
MIX_ROWS = 256
MIX_SEQS = 2
DEC_SEQS = 4
GATE_COLS = 256


def _dot(a, b):
    return jnp.dot(a, b, preferred_element_type=F32)


def _dot_nt(a, b):
    return lax.dot_general(a, b, (((1,), (1,)), ((), ())), preferred_element_type=F32)


def _rmsnorm(x, g):
    return x * lax.rsqrt(jnp.mean(x * x, axis=-1, keepdims=True) + EPS) * g


def _norm_rows(x, g):
    r = lax.rsqrt(jnp.mean(x * x, axis=-1, keepdims=True) + EPS)
    xg = x * g
    return (xg * r).astype(BF16), xg.astype(BF16), r


def _const_spec(shape):
    nd = len(shape)
    return pl.BlockSpec(shape, lambda *_: (0,) * nd, pipeline_mode=pl.Buffered(1))


def _ffn_rows(x_ref, o_ref, g_ref, w1_ref, w3_ref, w2_ref, gf_ref, final_norm, before_chunk=None):
    x = x_ref[...]
    h, h0, r = _norm_rows(x, g_ref[...])
    acc = None
    for c in range(D_FF // FFN_COLS):
        sl = slice(c * FFN_COLS, (c + 1) * FFN_COLS)
        if before_chunk is not None:
            before_chunk(c)
        w1c, w3c, w2c = (w1_ref[:, sl].astype(BF16), w3_ref[:, sl].astype(BF16), w2_ref[sl, :].astype(BF16))
        if c == 0:
            a = r * _dot(h0, w1c)
            b = r * _dot(h0, w3c)
        else:
            a = _dot(h, w1c)
            b = _dot(h, w3c)
        act = (a * jax.nn.sigmoid(a) * b).astype(BF16)
        part = _dot(act, w2c)
        if before_chunk is None:
            acc = part if acc is None else acc + part
        elif c == 0:
            o_ref[...] = part
        else:
            o_ref[...] += part
    y = x_ref[...] + 0.5 * (o_ref[...] if acc is None else acc)
    if final_norm:
        y = _rmsnorm(y, gf_ref[...])
    o_ref[...] = y


def _ffn_chunk_copies(hbm, stage, sem, c):
    (w1_hbm, w3_hbm, w2_hbm), (s1_ref, s3_ref, s2_ref) = hbm, stage
    sl = pl.ds(c * FFN_COLS, FFN_COLS)
    slot = c % FFN_STAGE_SLOTS
    return (pltpu.make_async_copy(w1_hbm.at[:, sl], s1_ref.at[slot], sem.at[slot, 0]),
            pltpu.make_async_copy(w3_hbm.at[:, sl], s3_ref.at[slot], sem.at[slot, 1]),
            pltpu.make_async_copy(w2_hbm.at[sl, :], s2_ref.at[slot], sem.at[slot, 2]))


def _ffn_kernel(xa_ref, xb_ref, g_ref, w1_hbm, w3_hbm, w2_hbm, gf_ref, oa_ref, ob_ref,
                w1_ref, w3_ref, w2_ref, s1_ref, s3_ref, s2_ref, sem, *, steps_a, final_norm):
    i = pl.program_id(0)
    n_chunks = D_FF // FFN_COLS
    weights = (w1_ref, w3_ref, w2_ref)
    stage = (s1_ref, s3_ref, s2_ref)
    copies = functools.partial(_ffn_chunk_copies, (w1_hbm, w3_hbm, w2_hbm), stage, sem)

    @pl.when(i == 0)
    def _():
        for c in range(min(FFN_STAGE_SLOTS, n_chunks)):
            for cp in copies(c):
                cp.start()

        def arrive(c):
            sl = slice(c * FFN_COLS, (c + 1) * FFN_COLS)
            slot = c % FFN_STAGE_SLOTS
            for cp in copies(c):
                cp.wait()
            w1_ref[:, sl] = s1_ref[slot].astype(BF16)
            w3_ref[:, sl] = s3_ref[slot].astype(BF16)
            w2_ref[sl, :] = s2_ref[slot].astype(BF16)
            if c + FFN_STAGE_SLOTS < n_chunks:
                for cp in copies(c + FFN_STAGE_SLOTS):
                    cp.start()

        _ffn_rows(xa_ref, oa_ref, g_ref, *weights, gf_ref, final_norm, before_chunk=arrive)

    @pl.when((i > 0) & (i < steps_a))
    def _():
        _ffn_rows(xa_ref, oa_ref, g_ref, *weights, gf_ref, final_norm)

    @pl.when(i >= steps_a)
    def _():
        _ffn_rows(xb_ref, ob_ref, g_ref, *weights, gf_ref, final_norm)


def _ffn_kernel_resident(xa_ref, xb_ref, g_ref, w1_ref, w3_ref, w2_ref, gf_ref, oa_ref, ob_ref, *,
                         steps_a, final_norm):
    i = pl.program_id(0)

    @pl.when(i < steps_a)
    def _():
        _ffn_rows(xa_ref, oa_ref, g_ref, w1_ref, w3_ref, w2_ref, gf_ref, final_norm)

    @pl.when(i >= steps_a)
    def _():
        _ffn_rows(xb_ref, ob_ref, g_ref, w1_ref, w3_ref, w2_ref, gf_ref, final_norm)


def _ffn(xa, xb, g, w1, w3, w2, g_final, *, final_norm, staged):
    rows = FFN_ROWS
    na, nb = xa.shape[0], xb.shape[0]
    assert na % rows == 0 and nb % rows == 0
    steps_a, steps_b = na // rows, nb // rows
    assert steps_a >= 1
    a_spec = pl.BlockSpec((rows, D_MODEL), lambda i: (jnp.minimum(i, steps_a - 1), 0))
    b_spec = pl.BlockSpec((rows, D_MODEL), lambda i: (jnp.maximum(i - steps_a, 0), 0))
    if not staged:
        return pl.pallas_call(
            functools.partial(_ffn_kernel_resident, steps_a=steps_a, final_norm=final_norm),
            grid=(steps_a + steps_b,),
            in_specs=[a_spec, b_spec, _const_spec((1, D_MODEL)), _const_spec((D_MODEL, D_FF)),
                      _const_spec((D_MODEL, D_FF)), _const_spec((D_FF, D_MODEL)), _const_spec((1, D_MODEL))],
            out_specs=[a_spec, b_spec],
            out_shape=[jax.ShapeDtypeStruct((na, D_MODEL), F32), jax.ShapeDtypeStruct((nb, D_MODEL), F32)],
            compiler_params=pltpu.CompilerParams(dimension_semantics=("arbitrary",),
                                                 vmem_limit_bytes=VMEM_LIMIT_BYTES),
            name="ffn_final" if final_norm else "ffn",
        )(xa, xb, g, w1, w3, w2, g_final)
    hbm_spec = pl.BlockSpec(memory_space=pl.ANY)
    slots = FFN_STAGE_SLOTS
    return pl.pallas_call(
        functools.partial(_ffn_kernel, steps_a=steps_a, final_norm=final_norm),
        grid=(steps_a + steps_b,),
        in_specs=[a_spec, b_spec, _const_spec((1, D_MODEL)), hbm_spec, hbm_spec, hbm_spec,
                  _const_spec((1, D_MODEL))],
        out_specs=[a_spec, b_spec],
        out_shape=[jax.ShapeDtypeStruct((na, D_MODEL), F32), jax.ShapeDtypeStruct((nb, D_MODEL), F32)],
        scratch_shapes=[pltpu.VMEM((D_MODEL, D_FF), BF16), pltpu.VMEM((D_MODEL, D_FF), BF16),
                        pltpu.VMEM((D_FF, D_MODEL), BF16),
                        pltpu.VMEM((slots, D_MODEL, FFN_COLS), F32), pltpu.VMEM((slots, D_MODEL, FFN_COLS), F32),
                        pltpu.VMEM((slots, FFN_COLS, D_MODEL), F32), pltpu.SemaphoreType.DMA((slots, 3))],
        compiler_params=pltpu.CompilerParams(dimension_semantics=("arbitrary",),
                                             vmem_limit_bytes=VMEM_LIMIT_BYTES),
        name="ffn_final" if final_norm else "ffn",
    )(xa, xb, g, w1, w3, w2, g_final)


def _lower_bound(lbl):
    m = jnp.max(lbl, axis=0, keepdims=True)
    e = jnp.exp(lbl - m)
    return e[0:1, :] / jnp.sum(e, axis=0, keepdims=True)


def _causal_in_chunk(rows, chunk, key_major=False):
    shift = chunk.bit_length() - 1
    assert 1 << shift == chunk
    r = lax.broadcasted_iota(jnp.int32, (rows, rows), 1 if key_major else 0)
    c = lax.broadcasted_iota(jnp.int32, (rows, rows), 0 if key_major else 1)
    return (lax.shift_right_logical(r, shift) == lax.shift_right_logical(c, shift)) & (c <= r)


def _chunk_cumsum(x, chunk):
    pos = lax.broadcasted_iota(jnp.int32, x.shape, 0) & (chunk - 1)
    step = 1
    while step < chunk:
        x = x + jnp.where(pos >= step, pltpu.roll(x, step, 0), 0.0)
        step *= 2
    return x


def _chunk_last(x, chunk):
    rows, width = x.shape
    return jnp.concatenate(
        [jnp.broadcast_to(x[n * chunk + chunk - 1:(n + 1) * chunk, :], (chunk, width))
         for n in range(rows // chunk)], axis=0)


def _gate_products(h, h0, r, win_ref):
    pf = r * _dot(h0, win_ref[:, 1 * D_A:2 * D_A].astype(BF16))
    pq = _dot(h, win_ref[:, 0 * D_A:1 * D_A].astype(BF16))
    pv = _dot(h, win_ref[:, 2 * D_A:3 * D_A].astype(BF16))
    og = _dot(h, win_ref[:, 3 * D_A:4 * D_A].astype(BF16))
    return pq, pf, pv, og


def _decay_chain(pq, pf, lb, chunk):
    one_m = 1.0 - lb
    sig = jax.nn.sigmoid(pf)
    f = lb + one_m * sig
    k_in = one_m * (1.0 - sig)
    b = _chunk_cumsum(jnp.log(f), chunk)
    blast = _chunk_last(b, chunk)
    q_d = pq * jnp.exp(b)
    k_d = k_in * jnp.exp(-b)
    k_w = k_in * jnp.exp(blast - b)
    return q_d, k_d, k_w, blast


def _intra(q_b, k_d_h, v_h, tril):
    scores = jnp.where(tril, _dot_nt(q_b, k_d_h.astype(BF16)), 0.0)
    return _dot(scores.astype(BF16), v_h.astype(BF16))


def _head_out(o, gh_h, og_h):
    o = o * lax.rsqrt(jnp.mean(o * o, axis=-1, keepdims=True) + EPS)
    return o * gh_h * (og_h * jax.nn.sigmoid(og_h))


def _chunk_blocks(x_h, rows, chunk):
    shift = chunk.bit_length() - 1
    rid = lax.shift_right_logical(lax.broadcasted_iota(jnp.int32, (rows, HEAD_DIM), 0), shift)
    xb = x_h.astype(BF16)
    zero = jnp.zeros_like(xb)
    return jnp.concatenate([jnp.where(rid == n, xb, zero) for n in range(rows // chunk)], axis=1)


def _conv_inputs(h, win_ref):
    bg = _dot(h, win_ref[:, 4 * D_A:4 * D_A + D_B].astype(BF16))
    cg = _dot(h, win_ref[:, 4 * D_A + D_B:4 * D_A + 2 * D_B].astype(BF16))
    vv = _dot(h, win_ref[:, 4 * D_A + 2 * D_B:4 * D_A + 3 * D_B].astype(BF16))
    return bg, cg * vv


def _gate_piece(h, win_ref, j):
    lo = 4 * D_A + 3 * D_B + j * GATE_COLS
    return jax.nn.sigmoid(_dot(h, win_ref[:, lo:lo + GATE_COLS].astype(BF16)))


def _merge_out(x, y_a_in, z, gates, wa_ref, wb_ref, wo_ref):
    half = D_MODEL // GATE_COLS
    y_a = _dot(y_a_in.astype(BF16), wa_ref[...].astype(BF16))
    y_b = _dot(z.astype(BF16), wb_ref[...].astype(BF16))
    merged = jnp.concatenate(gates[:half], axis=1) * y_a + jnp.concatenate(gates[half:], axis=1) * y_b
    return x + _dot(merged.astype(BF16), wo_ref[...].astype(BF16))


def _decode_head(hd, q_d, k_d, k_w, v, blast, tril, s0_ref, so_ref, nseq, chunk):
    rows = nseq * chunk
    sl = slice(hd * HEAD_DIM, (hd + 1) * HEAD_DIM)
    q_b = q_d[:, sl].astype(BF16)
    o_intra = _intra(q_b, k_d[:, sl], v[:, sl], tril)
    ds = _dot(k_w[:, sl].T.astype(BF16), _chunk_blocks(v[:, sl], rows, chunk))
    decay_t = jnp.exp(blast[:, sl]).T
    inter = []
    for j in range(nseq):
        s0 = s0_ref[0, j, hd]
        inter.append(_dot(q_b[j * chunk:(j + 1) * chunk], s0.astype(BF16)))
        so_ref[0, j, hd] = (s0 * decay_t[:, j * chunk:j * chunk + 1]
                            + ds[:, j * HEAD_DIM:(j + 1) * HEAD_DIM])
    return o_intra + jnp.concatenate(inter, axis=0)


def _decode_conv(u, c0, cw, nseq, chunk):
    tid = lax.broadcasted_iota(jnp.int32, (nseq, chunk, D_B), 1)
    u3 = u.reshape(nseq, chunk, D_B)
    r1 = pltpu.roll(u, 1, 0).reshape(nseq, chunk, D_B)
    r2 = pltpu.roll(u, 2, 0).reshape(nseq, chunk, D_B)
    u_m1 = jnp.where(tid == 0, c0[:, 1:2, :], r1)
    u_m2 = jnp.where(tid == 0, c0[:, 0:1, :], jnp.where(tid == 1, c0[:, 1:2, :], r2))
    conv = cw[0:1, :] * u_m2 + cw[1:2, :] * u_m1 + cw[2:3, :] * u3
    return conv.reshape(nseq * chunk, D_B), u3[:, chunk - 2:chunk, :]


def _mix_kernel(x_ref, xd_ref, s0_ref, c0_ref, lbl_ref, gm_ref, win_ref, cw_ref, gh_ref, wa_ref, wb_ref, wo_ref,
                xo_ref, so_ref, co_ref, xod_ref, sod_ref, cod_ref, st_scr, cv_scr, *, dec_len):
    nseq, rows, chunk = MIX_SEQS, MIX_ROWS, CHUNK
    prow = nseq * rows
    t = pl.program_id(1)

    @pl.when(t == 0)
    def _():
        st_scr[...] = jnp.zeros_like(st_scr)
        cv_scr[...] = jnp.zeros_like(cv_scr)

    x = jnp.concatenate([x_ref[...].reshape(prow, D_MODEL), xd_ref[...]], axis=0)
    h, h0, r = _norm_rows(x, gm_ref[...])
    lb = _lower_bound(lbl_ref[...])
    pq, pf, v, og = _gate_products(h, h0, r, win_ref)
    q_d, k_d, k_w, blast = _decay_chain(pq[:prow], pf[:prow], lb, chunk)
    qd_d, kd_d, kw_d, blast_d = _decay_chain(pq[prow:], pf[prow:], lb, dec_len)
    triu = _causal_in_chunk(rows, chunk, key_major=True)
    gh = gh_ref[...]
    bg, u = _conv_inputs(h, win_ref)
    cw = cw_ref[...]
    rid = lax.broadcasted_iota(jnp.int32, (rows, D_B), 0)

    units = [(s, hd, slice(s * rows, (s + 1) * rows), slice(hd * HEAD_DIM, (hd + 1) * HEAD_DIM))
             for s in range(nseq) for hd in range(N_HEADS)]
    q_b = [q_d[rs, sl].astype(BF16) for _, _, rs, sl in units]
    v_t = [v[rs, sl].T.astype(BF16) for _, _, rs, sl in units]
    scores_t = [_dot_nt(k_d[rs, sl].astype(BF16), q_b[i]) for i, (_, _, rs, sl) in enumerate(units)]
    ds_t = [_dot(v_t[i], _chunk_blocks(k_w[rs, sl], rows, chunk)) for i, (_, _, rs, sl) in enumerate(units)]
    o_intra = [_dot(v_t[i], jnp.where(triu, scores_t[i], 0.0).astype(BF16)).T for i in range(len(units))]
    s_t = [st_scr[s, hd] for s, hd, _, _ in units]
    inter = [[] for _ in units]
    assert 2 * D_MODEL // GATE_COLS == rows // chunk
    gates = []
    tril_d = _causal_in_chunk(DEC_SEQS * dec_len, dec_len)
    o_dec = []
    for n in range(rows // chunk):
        gates.append(_gate_piece(h, win_ref, n))
        if n < N_HEADS:
            o_dec.append(_decode_head(n, qd_d, kd_d, kw_d, v[prow:], blast_d, tril_d, s0_ref, sod_ref,
                                      DEC_SEQS, dec_len))
        for i, (_, _, rs, sl) in enumerate(units):
            inter[i].append(_dot(q_b[i][n * chunk:(n + 1) * chunk], s_t[i].T.astype(BF16)))
            decay = jnp.exp(blast[rs, sl][n * chunk:n * chunk + 1, :])
            s_t[i] = s_t[i] * decay + ds_t[i][:, n * HEAD_DIM:(n + 1) * HEAD_DIM]
    assert len(o_dec) == N_HEADS
    heads = []
    for i, (s, hd, rs, sl) in enumerate(units):
        st_scr[s, hd] = s_t[i]
        o = o_intra[i] + jnp.concatenate(inter[i], axis=0)
        heads.append(_head_out(o, gh[:, sl], og[rs, sl]))
    y_a_rows = [jnp.concatenate(heads[s * N_HEADS:(s + 1) * N_HEADS], axis=1) for s in range(nseq)]
    y_a_rows.append(jnp.concatenate(
        [_head_out(o_dec[hd], gh[:, hd * HEAD_DIM:(hd + 1) * HEAD_DIM], og[prow:, hd * HEAD_DIM:(hd + 1) * HEAD_DIM])
         for hd in range(N_HEADS)], axis=1))

    conv_rows = []
    for s in range(nseq):
        rs = slice(s * rows, (s + 1) * rows)
        us = u[rs]
        prev = cv_scr[s]
        u_m1 = jnp.where(rid == 0, prev[7:8, :], pltpu.roll(us, 1, 0))
        u_m2 = jnp.where(rid == 0, prev[6:7, :], jnp.where(rid == 1, prev[7:8, :], pltpu.roll(us, 2, 0)))
        conv_rows.append(cw[0:1, :] * u_m2 + cw[1:2, :] * u_m1 + cw[2:3, :] * us)
        cv_scr[s] = us[rows - 8:rows, :]
    conv_d, cod_ref[0] = _decode_conv(u[prow:], c0_ref[0], cw, DEC_SEQS, dec_len)
    conv_rows.append(conv_d)

    y_a_in = jnp.concatenate(y_a_rows, axis=0)
    z = bg * jnp.concatenate(conv_rows, axis=0)
    out = _merge_out(x, y_a_in, z, gates, wa_ref, wb_ref, wo_ref)
    xo_ref[...] = out[:prow].reshape(nseq, rows, D_MODEL)
    xod_ref[...] = out[prow:]

    @pl.when(t == pl.num_programs(1) - 1)
    def _():
        for s in range(nseq):
            for hd in range(N_HEADS):
                so_ref[0, s, hd] = st_scr[s, hd].T
            co_ref[0, s] = u[(s + 1) * rows - 2:(s + 1) * rows, :]


def _mix(x, xd, s0, c0, lbl, gm, win, cw, gh, wa, wb, wo):
    bsz, seq, _ = x.shape
    dbsz, dec_len, _ = xd.shape
    nseq, rows = MIX_SEQS, MIX_ROWS
    tiles = seq // rows
    assert seq % rows == 0 and bsz % nseq == 0
    assert dbsz == DEC_SEQS * (bsz // nseq) * tiles, "decode sequences are spread evenly over the grid steps"
    assert dec_len % 8 == 0 and dec_len >= CONV_W - 1
    n_in = win.shape[1]
    drows = DEC_SEQS * dec_len
    x_spec = pl.BlockSpec((nseq, rows, D_MODEL), lambda b, t: (b, t, 0))
    xd_spec = pl.BlockSpec((drows, D_MODEL), lambda b, t: (b * tiles + t, 0))
    sd_spec = pl.BlockSpec((1, DEC_SEQS, N_HEADS, HEAD_DIM, HEAD_DIM), lambda b, t: (0, b * tiles + t, 0, 0, 0))
    cd_spec = pl.BlockSpec((1, DEC_SEQS, CONV_W - 1, D_B), lambda b, t: (0, b * tiles + t, 0, 0))
    xo, so, co, xod, sod, cod = pl.pallas_call(
        functools.partial(_mix_kernel, dec_len=dec_len),
        grid=(bsz // nseq, tiles),
        in_specs=[x_spec, xd_spec, sd_spec, cd_spec, _const_spec(lbl.shape), _const_spec((1, D_MODEL)),
                  _const_spec((D_MODEL, n_in)), _const_spec((CONV_W, D_B)), _const_spec((1, D_A)),
                  _const_spec((D_A, D_MODEL)), _const_spec((D_B, D_MODEL)), _const_spec((D_MODEL, D_MODEL))],
        out_specs=[x_spec,
                   pl.BlockSpec((1, nseq, N_HEADS, HEAD_DIM, HEAD_DIM), lambda b, t: (0, b, 0, 0, 0)),
                   pl.BlockSpec((1, nseq, CONV_W - 1, D_B), lambda b, t: (0, b, 0, 0)),
                   xd_spec, sd_spec, cd_spec],
        out_shape=[jax.ShapeDtypeStruct(x.shape, F32),
                   jax.ShapeDtypeStruct((1, bsz, N_HEADS, HEAD_DIM, HEAD_DIM), F32),
                   jax.ShapeDtypeStruct((1, bsz, CONV_W - 1, D_B), F32),
                   jax.ShapeDtypeStruct((dbsz * dec_len, D_MODEL), F32),
                   jax.ShapeDtypeStruct(s0.shape, F32), jax.ShapeDtypeStruct(c0.shape, F32)],
        scratch_shapes=[pltpu.VMEM((nseq, N_HEADS, HEAD_DIM, HEAD_DIM), F32), pltpu.VMEM((nseq, 8, D_B), F32)],
        compiler_params=pltpu.CompilerParams(dimension_semantics=("arbitrary", "arbitrary"),
                                             vmem_limit_bytes=VMEM_LIMIT_BYTES),
        name="mix",
    )(x, xd.reshape(dbsz * dec_len, D_MODEL), s0, c0, lbl, gm, win, cw, gh, wa, wb, wo)
    return xo, so, co, xod.reshape(xd.shape), sod, cod


def kernel(x_prompt, x_sample, state_hgrn, state_conv, lower_bound_logits, g_ffn1, w1_ffn1, w3_ffn1, w2_ffn1,
           g_mix, w_in, conv_w, g_hgrn_out, w_a_out, w_b_out, w_o, g_ffn2, w1_ffn2, w3_ffn2, w2_ffn2, g_final):
    assert state_hgrn.shape[0] == 1, "single-layer trunk"
    sq = lambda w: w.reshape(w.shape[1:])
    ffn1 = (g_ffn1, sq(w1_ffn1), sq(w3_ffn1), sq(w2_ffn1))
    ffn2 = (g_ffn2, sq(w1_ffn2), sq(w3_ffn2), sq(w2_ffn2))
    gfin = g_final.reshape(1, D_MODEL)
    mixw = (lower_bound_logits, g_mix, sq(w_in), sq(conv_w), g_hgrn_out, sq(w_a_out), sq(w_b_out), sq(w_o))

    def ffn(xa, xb, weights, final_norm):
        ya, yb = _ffn(xa.reshape(-1, D_MODEL), xb.reshape(-1, D_MODEL), *weights, gfin, final_norm=final_norm,
                      staged=False)
        return ya.reshape(xa.shape), yb.reshape(xb.shape)

    xp, xs = ffn(x_prompt, x_sample, ffn1, False)
    xp, sh_p, sc_p, xs, sh_s, sc_s = _mix(xp, xs, state_hgrn, state_conv, *mixw)
    yp, ys = ffn(xp, xs, ffn2, True)
    return yp, ys, sh_p, sc_p, sh_s, sc_s
```
